```python
import functools
import jax, jax.numpy as jnp
from jax import lax
import numpy as np


D_MODEL = 2048
BATCH = 16
SEQ = 2048
DEPTH = 1
DEC_BATCH = 8
DEC_SEQ = 64
PAST_LEN = 4096

CHUNK = 64
N_PREV_CHUNKS = 8
BAND_ROWS = N_PREV_CHUNKS * CHUNK
D_MIX = D_MODEL
D_ATTN = D_MIX // 2
D_MLSTM = D_MIX - D_ATTN
HEAD_DIM_A = 64
N_HEADS_A = D_ATTN // HEAD_DIM_A
HEAD_DIM_B = 128
N_HEADS_B = D_MLSTM // HEAD_DIM_B
REL_CLIP = 256
CONV_W = 4
D_FF = -(-8 * D_MODEL // (3 * 256)) * 256
D_IN_PROJ = 3 * D_ATTN + 4 * D_MLSTM + 2 * N_HEADS_B
DN_ALPHA = (2.0 * DEPTH) ** 0.25
DN_BETA = (8.0 * DEPTH) ** -0.25

kernel_name = 'hybrid_chunkband_mlstm_stream_step'


def layer_norm(x, g, b, eps=1e-5):
    xf = x.astype(jnp.float32)
    mu = jnp.mean(xf, axis=-1, keepdims=True)
    var = jnp.mean(jnp.square(xf - mu), axis=-1, keepdims=True)
    return ((xf - mu) * lax.rsqrt(var + eps) * g + b).astype(x.dtype)


def head_rms_norm(h, g, eps=1e-6):
    hf = h.astype(jnp.float32)
    hf = hf * lax.rsqrt(jnp.mean(hf * hf, axis=-1, keepdims=True) + eps)
    return (hf * g.reshape(h.shape[-2], h.shape[-1])).astype(h.dtype)


def split_in_proj(z):
    sizes = [D_ATTN, D_ATTN, D_ATTN, 2 * D_MLSTM, D_MLSTM, D_MLSTM, N_HEADS_B, N_HEADS_B]
    return jnp.split(z, [int(s) for s in np.cumsum(sizes)[:-1]], axis=-1)


def attend(q, k, v, bias, valid):
    s = jnp.einsum('bqhd,bkhd->bhqk', q, k).astype(jnp.float32) * (HEAD_DIM_A ** -0.5)
    s = s + bias[None].astype(jnp.float32)
    if valid is not None:
        s = jnp.where(valid[None, None, None, :], s, -jnp.inf)
    p = jax.nn.softmax(s, axis=-1)
    return jnp.einsum('bhqk,bkhd->bqhd', p.astype(v.dtype), v)


def chunk_band_attention_prompt(q, k, v, rel_bias):
    B, S, H, Dh = q.shape
    n_chunks = S // CHUNK
    band = BAND_ROWS + CHUNK
    pad = jnp.zeros((B, BAND_ROWS, H, Dh), k.dtype)
    k_pad = jnp.concatenate([pad, k], axis=1)
    v_pad = jnp.concatenate([pad, v], axis=1)
    rel = jnp.arange(CHUNK)[:, None] + BAND_ROWS - jnp.arange(band)[None, :]
    bias = rel_bias[:, jnp.clip(rel, -REL_CLIP, REL_CLIP) + REL_CLIP]

    def one_chunk(c):
        start = c * CHUNK
        qc = lax.dynamic_slice_in_dim(q, start, CHUNK, axis=1)
        kc = lax.dynamic_slice_in_dim(k_pad, start, band, axis=1)
        vc = lax.dynamic_slice_in_dim(v_pad, start, band, axis=1)
        valid = (start - BAND_ROWS + jnp.arange(band)) >= 0
        return attend(qc, kc, vc, bias, valid)

    out = lax.map(one_chunk, jnp.arange(n_chunks))
    out = jnp.moveaxis(out, 0, 1).reshape(B, S, H, Dh)
    n_keep = min(BAND_ROWS, S)
    return out, k[:, S - n_keep:], v[:, S - n_keep:]


def chunk_band_attention_sample(q, k, v, cache_k, cache_v, rel_bias):
    B, L, H, Dh = q.shape
    n_past = cache_k.shape[1]
    k_all = jnp.concatenate([cache_k.astype(k.dtype), k], axis=1)
    v_all = jnp.concatenate([cache_v.astype(v.dtype), v], axis=1)
    key_off = jnp.concatenate([jnp.arange(n_past) - n_past, jnp.arange(L)])
    rel = jnp.arange(L)[:, None] - key_off[None, :]
    bias = rel_bias[:, jnp.clip(rel, -REL_CLIP, REL_CLIP) + REL_CLIP]
    out = attend(q, k_all, v_all, bias, None)
    n_keep = min(BAND_ROWS, n_past + L)
    return out, k_all[:, n_past + L - n_keep:], v_all[:, n_past + L - n_keep:]


def mlstm_chunk(state, inputs):
    C, n, m = state
    q, k, v, ig, lf = inputs
    L = q.shape[1]
    F = jnp.cumsum(lf, axis=1)
    a = F + m[:, None, :]
    causal = jnp.tril(jnp.ones((L, L), dtype=bool))
    d = jnp.where(causal[None, :, :, None],
                  F[:, :, None, :] - F[:, None, :, :] + ig[:, None, :, :], -jnp.inf)
    m_t = jnp.maximum(a, jnp.max(d, axis=2))
    w_inter = jnp.exp(a - m_t)
    s = jnp.einsum('bthd,bjhd->btjh', q, k) * jnp.exp(d - m_t[:, :, None, :])
    num = jnp.einsum('btjh,bjhe->bthe', s, v) + w_inter[..., None] * jnp.einsum('bthd,bhde->bthe', q, C)
    den = jnp.sum(s, axis=2) + w_inter * jnp.einsum('bthd,bhd->bth', q, n)
    h = num / jnp.maximum(jnp.abs(den), jnp.exp(-m_t))[..., None]
    F_L = F[:, -1]
    g = F_L[:, None, :] - F + ig
    a_L = F_L + m
    m_new = jnp.maximum(a_L, jnp.max(g, axis=1))
    w_old = jnp.exp(a_L - m_new)
    w_k = jnp.exp(g - m_new[:, None, :])
    C_new = w_old[..., None, None] * C + jnp.einsum('bjh,bjhd,bjhe->bhde', w_k, k, v)
    n_new = w_old[..., None] * n + jnp.einsum('bjh,bjhd->bhd', w_k, k)
    return (C_new, n_new, m_new), h


def mlstm_mixer(qk_raw, v, i_pre, f_pre, conv_buf, state, w_conv, b_conv, b_igate, b_fgate):
    B, L, _ = qk_raw.shape
    qk_pad = jnp.concatenate([conv_buf.astype(qk_raw.dtype), qk_raw], axis=1)
    qk = b_conv + sum(qk_pad[:, j:j + L] * w_conv[j] for j in range(CONV_W))
    qk = jax.nn.silu(qk).astype(jnp.float32)
    q = qk[..., :D_MLSTM].reshape(B, L, N_HEADS_B, HEAD_DIM_B)
    k = qk[..., D_MLSTM:].reshape(B, L, N_HEADS_B, HEAD_DIM_B) * (HEAD_DIM_B ** -0.5)
    vf = v.astype(jnp.float32).reshape(B, L, N_HEADS_B, HEAD_DIM_B)
    ig = (i_pre + b_igate).astype(jnp.float32)
    lf = jax.nn.log_sigmoid((f_pre + b_fgate).astype(jnp.float32))
    state = tuple(s.astype(jnp.float32) for s in state)
    if L > CHUNK:
        n_chunks = L // CHUNK
        to_chunks = lambda t: jnp.moveaxis(t.reshape(B, n_chunks, CHUNK, *t.shape[2:]), 1, 0)
        new_state, h = lax.scan(mlstm_chunk, state, tuple(to_chunks(t) for t in (q, k, vf, ig, lf)))
        h = jnp.moveaxis(h, 0, 1).reshape(B, L, N_HEADS_B, HEAD_DIM_B)
    else:
        new_state, h = mlstm_chunk(state, (q, k, vf, ig, lf))
    return h, new_state, qk_pad[:, L:]


def trunk_layer(x, attention, conv_buf, mlstm_state, w_in, b_igate, b_fgate, w_conv, b_conv,
                g_attn_norm, g_mlstm_norm, w_out, ln1_g, ln1_b, w_ffn_gate, w_ffn_up, w_ffn_down,
                ln2_g, ln2_b):
    B, L, _ = x.shape
    z = x @ w_in
    qa, ka, va, qk_raw, vb, ob, ib, fb = split_in_proj(z)
    to_heads = lambda t: t.reshape(B, L, N_HEADS_A, HEAD_DIM_A)
    attn, new_k, new_v = attention(to_heads(qa), to_heads(ka), to_heads(va))
    h_b, new_state, new_conv = mlstm_mixer(qk_raw, vb, ib, fb, conv_buf, mlstm_state,
                                           w_conv, b_conv, b_igate, b_fgate)
    attn_out = head_rms_norm(attn, g_attn_norm).reshape(B, L, D_ATTN)
    mlstm_out = (head_rms_norm(h_b, g_mlstm_norm).reshape(B, L, D_MLSTM)
                 * jax.nn.sigmoid(ob)).astype(x.dtype)
    mix = jnp.concatenate([attn_out, mlstm_out], axis=-1) @ w_out
    x1 = layer_norm(DN_ALPHA * x + mix, ln1_g, ln1_b)
    ffn = (jax.nn.silu(x1 @ w_ffn_gate) * (x1 @ w_ffn_up)) @ w_ffn_down
    y = layer_norm(DN_ALPHA * x1 + ffn, ln2_g, ln2_b)
    return y, new_k, new_v, new_conv, new_state


def setup_inputs(seed: int = 0) -> dict:
    key = jax.random.key(seed)
    ks = jax.random.split(key, 24)
    nrm = lambda k, shape, scale: scale * jax.random.normal(k, shape, jnp.float32)
    band = min(BAND_ROWS, PAST_LEN)
    return {
        'x_prompt': nrm(ks[0], (BATCH, SEQ, D_MODEL), 1.0),
        'x_sample': nrm(ks[1], (DEC_BATCH, DEC_SEQ, D_MODEL), 1.0),
        'cache_attn_k': nrm(ks[2], (DEPTH, DEC_BATCH, band, N_HEADS_A, HEAD_DIM_A), 1.0),
        'cache_attn_v': nrm(ks[3], (DEPTH, DEC_BATCH, band, N_HEADS_A, HEAD_DIM_A), 1.0),
        'state_conv': nrm(ks[4], (DEPTH, DEC_BATCH, CONV_W - 1, 2 * D_MLSTM), 1.0),
        'state_mlstm_C': nrm(ks[5], (DEPTH, DEC_BATCH, N_HEADS_B, HEAD_DIM_B, HEAD_DIM_B), 0.3),
        'state_mlstm_n': nrm(ks[6], (DEPTH, DEC_BATCH, N_HEADS_B, HEAD_DIM_B), 0.3),
        'state_mlstm_m': nrm(ks[7], (DEPTH, DEC_BATCH, N_HEADS_B), 1.0),
        'w_in': nrm(ks[8], (DEPTH, D_MODEL, D_IN_PROJ), D_MODEL ** -0.5),
        'b_igate': nrm(ks[9], (DEPTH, N_HEADS_B), 0.5),
        'b_fgate': 3.0 + nrm(ks[10], (DEPTH, N_HEADS_B), 0.5),
        'w_conv': nrm(ks[11], (DEPTH, CONV_W, 2 * D_MLSTM), CONV_W ** -0.5),
        'b_conv': nrm(ks[12], (DEPTH, 2 * D_MLSTM), 0.01),
        'rel_bias': nrm(ks[13], (DEPTH, N_HEADS_A, 2 * REL_CLIP + 1), 0.5),
        'g_attn_norm': 1.0 + nrm(ks[14], (DEPTH, D_ATTN), 0.01),
        'g_mlstm_norm': 1.0 + nrm(ks[15], (DEPTH, D_MLSTM), 0.01),
        'w_out': nrm(ks[16], (DEPTH, D_MIX, D_MODEL), DN_BETA * D_MIX ** -0.5),
        'ln1_g': 1.0 + nrm(ks[17], (DEPTH, D_MODEL), 0.01),
        'ln1_b': nrm(ks[18], (DEPTH, D_MODEL), 0.01),
        'w_ffn_gate': nrm(ks[19], (DEPTH, D_MODEL, D_FF), D_MODEL ** -0.5),
        'w_ffn_up': nrm(ks[20], (DEPTH, D_MODEL, D_FF), D_MODEL ** -0.5),
        'w_ffn_down': nrm(ks[21], (DEPTH, D_FF, D_MODEL), DN_BETA * D_FF ** -0.5),
        'ln2_g': 1.0 + nrm(ks[22], (DEPTH, D_MODEL), 0.01),
        'ln2_b': nrm(ks[23], (DEPTH, D_MODEL), 0.01),
    }


def reference(x_prompt, x_sample, cache_attn_k, cache_attn_v, state_conv, state_mlstm_C,
              state_mlstm_n, state_mlstm_m, w_in, b_igate, b_fgate, w_conv, b_conv, rel_bias,
              g_attn_norm, g_mlstm_norm, w_out, ln1_g, ln1_b, w_ffn_gate, w_ffn_up, w_ffn_down,
              ln2_g, ln2_b):
    xp, xs = x_prompt, x_sample
    bp = x_prompt.shape[0]
    new_p, new_s = [], []
    for l in range(DEPTH):
        weights = (w_in[l], b_igate[l], b_fgate[l], w_conv[l], b_conv[l], g_attn_norm[l],
                   g_mlstm_norm[l], w_out[l], ln1_g[l], ln1_b[l], w_ffn_gate[l], w_ffn_up[l],
                   w_ffn_down[l], ln2_g[l], ln2_b[l])
        zero_conv = jnp.zeros((bp, CONV_W - 1, 2 * D_MLSTM), xp.dtype)
        zero_state = (jnp.zeros((bp, N_HEADS_B, HEAD_DIM_B, HEAD_DIM_B), jnp.float32),
                      jnp.zeros((bp, N_HEADS_B, HEAD_DIM_B), jnp.float32),
                      jnp.zeros((bp, N_HEADS_B), jnp.float32))
        attn_p = functools.partial(chunk_band_attention_prompt, rel_bias=rel_bias[l])
        xp, kp, vp, cp, (Cp, n_p, mp) = trunk_layer(xp, attn_p, zero_conv, zero_state, *weights)
        new_p.append((kp, vp, cp, Cp, n_p, mp))
        attn_s = functools.partial(chunk_band_attention_sample, cache_k=cache_attn_k[l],
                                   cache_v=cache_attn_v[l], rel_bias=rel_bias[l])
        xs, ks_, vs, cs, (Cs, n_s, ms) = trunk_layer(
            xs, attn_s, state_conv[l], (state_mlstm_C[l], state_mlstm_n[l], state_mlstm_m[l]), *weights)
        new_s.append((ks_, vs, cs, Cs, n_s, ms))
    k_p, v_p, conv_p, C_p, n_p, m_p = [jnp.stack(t) for t in zip(*new_p)]
    k_s, v_s, conv_s, C_s, n_s, m_s = [jnp.stack(t) for t in zip(*new_s)]
    return (xp, xs, k_p, v_p, conv_p, C_p, n_p, m_p, k_s, v_s, conv_s, C_s, n_s, m_s)
```

```python
import functools

import jax
import jax.numpy as jnp
from jax import lax
from jax.experimental import pallas as pl
from jax.experimental.pallas import tpu as pltpu

CHUNK = 64
BAND_ROWS = 8 * CHUNK
BAND = BAND_ROWS + CHUNK
REL_CLIP = 256
HEAD_DIM_A = 64
HEAD_DIM_B = 128
CONV_W = 4
LANES = 128
SUBLANES = 8
NEG_BIG = -1e30
VMEM_LIMIT_BYTES = 56 * 1024 * 1024

F32 = jnp.float32
BF16 = jnp.bfloat16


def _params(*semantics):
    return pltpu.CompilerParams(dimension_semantics=semantics, vmem_limit_bytes=VMEM_LIMIT_BYTES)


def _mm(a, b):
    return jnp.dot(a, b, preferred_element_type=F32)


def _mm_nt(a, b, precision=None):
    return lax.dot_general(a, b, (((1,), (1,)), ((), ())), precision=precision,
                           preferred_element_type=F32)


def _in_proj_kernel(x_ref, w_ref, wg_ref, z_ref, g_ref, xb_ref):
    @pl.when(pl.program_id(1) == 0)
    def _():
        xb = x_ref[...].astype(BF16)
        xb_ref[...] = xb
        g_ref[...] = _mm(xb, wg_ref[...])

    z_ref[...] = _mm(xb_ref[...], w_ref[...]).astype(z_ref.dtype)


def _in_proj(x2d, w_main, w_gate, bm, bn):
    t, d = x2d.shape
    n = w_main.shape[1]
    return pl.pallas_call(
        _in_proj_kernel,
        grid=(t // bm, n // bn),
        in_specs=[pl.BlockSpec((bm, d), lambda i, j: (i, 0)),
                  pl.BlockSpec((d, bn), lambda i, j: (0, j)),
                  pl.BlockSpec((d, LANES), lambda i, j: (0, 0))],
        out_specs=[pl.BlockSpec((bm, bn), lambda i, j: (i, j)),
                   pl.BlockSpec((bm, LANES), lambda i, j: (i, 0))],
        out_shape=[jax.ShapeDtypeStruct((t, n), BF16),
                   jax.ShapeDtypeStruct((t, LANES), F32)],
        scratch_shapes=[pltpu.VMEM((bm, d), BF16)],
        compiler_params=_params("parallel", "arbitrary"),
        name="in_proj",
    )(x2d, w_main, w_gate)


def _attn_kernel(*refs, has_hist, n_pairs):
    if has_hist:
        q_ref, k_ref, v_ref, kh_ref, vh_ref, bias_ref, g_ref, o_ref, kpad, vpad = refs
    else:
        q_ref, k_ref, v_ref, bias_ref, g_ref, o_ref, kpad, vpad = refs
    c = pl.program_id(1)
    s_len = k_ref.shape[1]

    @pl.when(c == 0)
    def _():
        if has_hist:
            kpad[0:BAND_ROWS, :] = kh_ref[0]
            vpad[0:BAND_ROWS, :] = vh_ref[0]
        else:
            kpad[0:BAND_ROWS, :] = jnp.zeros((BAND_ROWS, kpad.shape[1]), BF16)
            vpad[0:BAND_ROWS, :] = jnp.zeros((BAND_ROWS, vpad.shape[1]), BF16)
        kpad[BAND_ROWS:BAND_ROWS + s_len, :] = k_ref[0]
        vpad[BAND_ROWS:BAND_ROWS + s_len, :] = v_ref[0]

    start = pl.multiple_of(c * CHUNK, CHUNK)
    lane = lax.broadcasted_iota(jnp.int32, (CHUNK, LANES), 1)
    row2 = lax.broadcasted_iota(jnp.int32, (2 * CHUNK, LANES), 0)
    lane2 = lax.broadcasted_iota(jnp.int32, (2 * CHUNK, LANES), 1)
    own_head = (row2 >= CHUNK) == (lane2 >= HEAD_DIM_A)
    if not has_hist:
        col = lax.broadcasted_iota(jnp.int32, (2 * CHUNK, BAND), 1)
        valid = col >= BAND_ROWS - c * CHUNK

    for p in range(n_pairs):
        cols = slice(p * LANES, (p + 1) * LANES)
        qp = q_ref[0, :, cols] * (HEAD_DIM_A ** -0.5)
        zero = jnp.zeros_like(qp)
        q2 = jnp.concatenate([jnp.where(lane < HEAD_DIM_A, qp, zero),
                              jnp.where(lane >= HEAD_DIM_A, qp, zero)], axis=0)
        kb = kpad[pl.ds(start, BAND), cols]
        vb = vpad[pl.ds(start, BAND), cols]
        s = _mm_nt(q2, kb) + bias_ref[p]
        if not has_hist:
            s = jnp.where(valid, s, NEG_BIG)
        e = jnp.exp(s - jnp.max(s, axis=-1, keepdims=True))
        denom = jnp.sum(e, axis=-1, keepdims=True)
        o = _mm(e.astype(BF16), vb) / denom
        o = jnp.where(own_head, o, 0.0)
        o = o * lax.rsqrt(jnp.sum(o * o, axis=-1, keepdims=True) * (1.0 / HEAD_DIM_A) + 1e-6)
        res = (o[:CHUNK] + o[CHUNK:]) * g_ref[:, cols]
        o_ref[0, :, cols] = res.astype(o_ref.dtype)


def _attention(z3, hist_k, hist_v, bias_pairs, g_attn):
    b, s_len, _ = z3.shape
    d_attn = g_attn.shape[1]
    n_pairs = d_attn // LANES
    has_hist = hist_k is not None
    in_specs = [pl.BlockSpec((1, CHUNK, d_attn), lambda i, c: (i, c, 0)),
                pl.BlockSpec((1, s_len, d_attn), lambda i, c: (i, 0, 1)),
                pl.BlockSpec((1, s_len, d_attn), lambda i, c: (i, 0, 2))]
    args = [z3, z3, z3]
    if has_hist:
        in_specs += [pl.BlockSpec((1, BAND_ROWS, d_attn), lambda i, c: (i, 0, 0))] * 2
        args += [hist_k, hist_v]
    in_specs += [pl.BlockSpec(bias_pairs.shape, lambda i, c: (0, 0, 0)),
                 pl.BlockSpec((1, d_attn), lambda i, c: (0, 0))]
    args += [bias_pairs, g_attn]
    return pl.pallas_call(
        functools.partial(_attn_kernel, has_hist=has_hist, n_pairs=n_pairs),
        grid=(b, s_len // CHUNK),
        in_specs=in_specs,
        out_specs=pl.BlockSpec((1, CHUNK, d_attn), lambda i, c: (i, c, 0)),
        out_shape=jax.ShapeDtypeStruct((b, s_len, d_attn), BF16),
        scratch_shapes=[pltpu.VMEM((BAND_ROWS + s_len, d_attn), BF16)] * 2,
        compiler_params=_params("parallel", "arbitrary"),
        name="attn_hist" if has_hist else "attn",
    )(*args)


def _bias_pairs(rel_bias):
    n_heads = rel_bias.shape[0]
    rel = jnp.arange(CHUNK)[:, None] + BAND_ROWS - jnp.arange(BAND)[None, :]
    bias = rel_bias[:, jnp.clip(rel, -REL_CLIP, REL_CLIP) + REL_CLIP]
    return bias.reshape(n_heads // 2, 2 * CHUNK, BAND).astype(F32)


def _log_sigmoid(x):
    return jnp.minimum(x, 0.0) - jnp.log1p(jnp.exp(-jnp.abs(x)))


def _mlstm_kernel(q_ref, k_ref, v_ref, og_ref, gates_ref, conv0_ref, c0_ref, n0_ref, m0_ref,
                  wconv_ref, bconv_ref, gbias_ref, gnorm_ref,
                  o_ref, c_out_ref, n_out_ref, m_out_ref,
                  xs, c_s, n_s, m_s, *, n_heads, blk):
    c = pl.program_id(1)
    last = pl.num_programs(1) - 1
    d_b = n_heads * HEAD_DIM_B
    pad = SUBLANES

    @pl.when(c == 0)
    def _():
        xs[0:pad, :] = conv0_ref[0]
        c_s[...] = c0_ref[0]
        n_s[...] = n0_ref[0]
        m_s[...] = m0_ref[0]

    xs[pad:pad + blk, 0:d_b] = q_ref[0].astype(F32)
    xs[pad:pad + blk, d_b:2 * d_b] = k_ref[0].astype(F32)

    gates = gates_ref[0] + gbias_ref[...]
    lane = lax.broadcasted_iota(jnp.int32, (blk, LANES), 1)
    row_i = lax.broadcasted_iota(jnp.int32, (blk, blk), 0)
    col_i = lax.broadcasted_iota(jnp.int32, (blk, blk), 1)
    causal = row_i >= col_i
    tril = jnp.where(causal, 1.0, 0.0).astype(F32)
    f_cum = jnp.dot(tril, _log_sigmoid(gates), precision=lax.Precision.HIGHEST,
                    preferred_element_type=F32)
    sel_r = lax.broadcasted_iota(jnp.int32, (SUBLANES, LANES), 0)
    sel_c = lax.broadcasted_iota(jnp.int32, (SUBLANES, LANES), 1)
    sel = (jnp.where(sel_c == sel_r, 1.0, 0.0) - jnp.where(sel_c == sel_r + n_heads, 1.0, 0.0)).astype(F32)
    ig_minus_f_rows = _mm_nt(sel, jnp.where(lane < n_heads, gates, f_cum),
                             precision=lax.Precision.HIGHEST)

    def conv_silu(col0):
        acc = bconv_ref[:, col0:col0 + HEAD_DIM_B]
        for j in range(CONV_W):
            acc = acc + xs[pl.ds(pad - (CONV_W - 1) + j, blk), col0:col0 + HEAD_DIM_B] * wconv_ref[j:j + 1, col0:col0 + HEAD_DIM_B]
        return acc * jax.nn.sigmoid(acc)

    for h in range(n_heads):
        cols = slice(h * HEAD_DIM_B, (h + 1) * HEAD_DIM_B)
        q = conv_silu(h * HEAD_DIM_B)
        k = conv_silu(d_b + h * HEAD_DIM_B) * (HEAD_DIM_B ** -0.5)
        v = v_ref[0, :, cols]
        qb = q.astype(BF16)
        kb = k.astype(BF16)

        ig = gates[:, h:h + 1]
        f_h = f_cum[:, n_heads + h:n_heads + h + 1]
        m_prev = m_s[h:h + 1, 0:1]
        a = f_h + m_prev
        d = jnp.where(causal, f_h + ig_minus_f_rows[h:h + 1, :], NEG_BIG)
        m_t = jnp.maximum(a, jnp.max(d, axis=-1, keepdims=True))
        w_inter = jnp.exp(a - m_t)
        s = _mm_nt(qb, kb) * jnp.exp(d - m_t)
        c_prev = c_s[h]
        n_prev = n_s[h:h + 1, :]
        num = _mm(s.astype(BF16), v) + w_inter * _mm(qb, c_prev.astype(BF16))
        den = jnp.sum(s, axis=-1, keepdims=True) + w_inter * jnp.sum(q * n_prev, axis=-1, keepdims=True)
        hid = num / jnp.maximum(jnp.abs(den), jnp.exp(-m_t))

        f_last = f_h[blk - 1:blk, :]
        g = f_last - f_h + ig
        a_last = f_last + m_prev
        m_new = jnp.maximum(a_last, jnp.max(g, axis=0, keepdims=True))
        w_old = jnp.exp(a_last - m_new)
        kw = k * jnp.exp(g - m_new)
        c_new = w_old * c_prev + lax.dot_general(kw.astype(BF16), v, (((0,), (0,)), ((), ())),
                                                 preferred_element_type=F32)
        n_new = w_old * n_prev + jnp.sum(kw, axis=0, keepdims=True)
        c_s[h] = c_new
        n_s[h:h + 1, :] = n_new
        m_s[h:h + 1, :] = jnp.broadcast_to(m_new, (1, LANES))

        hn = hid * lax.rsqrt(jnp.mean(hid * hid, axis=-1, keepdims=True) + 1e-6) * gnorm_ref[:, cols]
        o_ref[0, :, cols] = (hn * jax.nn.sigmoid(og_ref[0, :, cols].astype(F32))).astype(o_ref.dtype)

    xs[0:pad, :] = xs[blk:blk + pad, :]

    @pl.when(c == last)
    def _():
        c_out_ref[0] = c_s[...]
        n_out_ref[0] = n_s[...]
        m_out_ref[0] = m_s[...]


def _mlstm(z3, gates3, conv0, c0, n0, m0, w_conv, b_conv, gate_bias, g_norm, blk, col_block0):
    b, s_len, _ = z3.shape
    n_heads = c0.shape[1]
    d_b = n_heads * HEAD_DIM_B
    zspec = lambda k: pl.BlockSpec((1, blk, d_b), lambda i, c: (i, c, col_block0 + k))
    full = lambda a: pl.BlockSpec(a.shape, lambda i, c: (0,) * a.ndim)
    per_b = lambda a: pl.BlockSpec((1,) + a.shape[1:], lambda i, c: (i,) + (0,) * (a.ndim - 1))
    return pl.pallas_call(
        functools.partial(_mlstm_kernel, n_heads=n_heads, blk=blk),
        grid=(b, s_len // blk),
        in_specs=[zspec(0), zspec(1), zspec(2), zspec(3),
                  pl.BlockSpec((1, blk, LANES), lambda i, c: (i, c, 0)),
                  per_b(conv0), per_b(c0), per_b(n0), per_b(m0),
                  full(w_conv), full(b_conv), full(gate_bias), full(g_norm)],
        out_specs=[pl.BlockSpec((1, blk, d_b), lambda i, c: (i, c, 0)),
                   per_b(c0), per_b(n0), per_b(m0)],
        out_shape=[jax.ShapeDtypeStruct((b, s_len, d_b), BF16),
                   jax.ShapeDtypeStruct(c0.shape, F32),
                   jax.ShapeDtypeStruct(n0.shape, F32),
                   jax.ShapeDtypeStruct(m0.shape, F32)],
        scratch_shapes=[pltpu.VMEM((blk + SUBLANES, 2 * d_b), F32),
                        pltpu.VMEM(c0.shape[1:], F32),
                        pltpu.VMEM(n0.shape[1:], F32),
                        pltpu.VMEM(m0.shape[1:], F32)],
        compiler_params=_params("parallel", "arbitrary"),
        name="mlstm",
    )(z3, z3, z3, z3, gates3, conv0, c0, n0, m0, w_conv, b_conv, gate_bias, g_norm)


def _layer_norm(r, g, b):
    mu = jnp.mean(r, axis=-1, keepdims=True)
    rc = r - mu
    var = jnp.mean(rc * rc, axis=-1, keepdims=True)
    return rc * lax.rsqrt(var + 1e-5) * g + b


def _out_proj_kernel(a_ref, m_ref, x_ref, wa_ref, wm_ref, g_ref, b_ref, x1_ref, x1b_ref, *, alpha):
    mix = _mm(a_ref[...], wa_ref[...]) + _mm(m_ref[...], wm_ref[...])
    y = _layer_norm(alpha * x_ref[...] + mix, g_ref[...], b_ref[...])
    x1_ref[...] = y
    x1b_ref[...] = y.astype(BF16)


def _out_proj(attn2d, mlstm2d, x2d, w_out, ln_g, ln_b, alpha, bm):
    t, d = x2d.shape
    da, db = attn2d.shape[1], mlstm2d.shape[1]
    assert da == db
    row = lambda w: pl.BlockSpec((bm, w), lambda i: (i, 0))
    vec = pl.BlockSpec((1, d), lambda i: (0, 0))
    return pl.pallas_call(
        functools.partial(_out_proj_kernel, alpha=alpha),
        grid=(t // bm,),
        in_specs=[row(da), row(db), row(d),
                  pl.BlockSpec((da, d), lambda i: (0, 0)),
                  pl.BlockSpec((db, d), lambda i: (1, 0)),
                  vec, vec],
        out_specs=[row(d), row(d)],
        out_shape=[jax.ShapeDtypeStruct((t, d), F32), jax.ShapeDtypeStruct((t, d), BF16)],
        compiler_params=_params("parallel"),
        name="out_proj_ln",
    )(attn2d, mlstm2d, x2d, w_out, w_out, ln_g, ln_b)


def _ffn_up_kernel(x_ref, wg_ref, wu_ref, h_ref):
    x = x_ref[...]
    gate = _mm(x, wg_ref[...])
    h_ref[...] = (gate * jax.nn.sigmoid(gate) * _mm(x, wu_ref[...])).astype(h_ref.dtype)


def _ffn_up(x1b, w_gate, w_up, bm, bf):
    t, d = x1b.shape
    f = w_gate.shape[1]
    wspec = pl.BlockSpec((d, bf), lambda i, j: (0, j))
    return pl.pallas_call(
        _ffn_up_kernel,
        grid=(t // bm, f // bf),
        in_specs=[pl.BlockSpec((bm, d), lambda i, j: (i, 0)), wspec, wspec],
        out_specs=pl.BlockSpec((bm, bf), lambda i, j: (i, j)),
        out_shape=jax.ShapeDtypeStruct((t, f), BF16),
        compiler_params=_params("parallel", "arbitrary"),
        name="ffn_up",
    )(x1b, w_gate, w_up)


def _ffn_down_kernel(h_ref, wd_ref, x1_ref, g_ref, b_ref, y_ref, *, alpha):
    k = pl.program_id(1)
    part = _mm(h_ref[...], wd_ref[...])

    @pl.when(k == 0)
    def _():
        y_ref[...] = alpha * x1_ref[...] + part

    @pl.when(k > 0)
    def _():
        y_ref[...] += part

    @pl.when(k == pl.num_programs(1) - 1)
    def _():
        y_ref[...] = _layer_norm(y_ref[...], g_ref[...], b_ref[...])


def _ffn_down(h, w_down, x1, ln_g, ln_b, alpha, bm, bk):
    t, f = h.shape
    d = w_down.shape[1]
    vec = pl.BlockSpec((1, d), lambda i, k: (0, 0))
    return pl.pallas_call(
        functools.partial(_ffn_down_kernel, alpha=alpha),
        grid=(t // bm, f // bk),
        in_specs=[pl.BlockSpec((bm, bk), lambda i, k: (i, k)),
                  pl.BlockSpec((bk, d), lambda i, k: (k, 0)),
                  pl.BlockSpec((bm, d), lambda i, k: (i, 0)),
                  vec, vec],
        out_specs=pl.BlockSpec((bm, d), lambda i, k: (i, 0)),
        out_shape=jax.ShapeDtypeStruct((t, d), F32),
        compiler_params=_params("parallel", "arbitrary"),
        name="ffn_down_ln",
    )(h, w_down, x1, ln_g, ln_b)


def _largest_divisor(n, cap):
    return max(d for d in range(1, min(n, cap) + 1) if n % d == 0)


def _trunk_layer(x, hist_k, hist_v, conv_state, mlstm_state, w, alpha, mlstm_blk):
    b, s_len, d = x.shape
    t = b * s_len
    c0, n0, m0 = mlstm_state
    n_heads_b = c0.shape[1]
    d_b = n_heads_b * HEAD_DIM_B
    d_attn = w["g_attn"].shape[1]
    assert d_attn == d_b and w["w_main"].shape[1] == 3 * d_attn + 4 * d_b

    bm = _largest_divisor(t, 1024)
    x2d = x.reshape(t, d)
    z, gates = _in_proj(x2d, w["w_main"], w["w_gate"], bm, 1024)
    z3 = z.reshape(b, s_len, -1)

    attn = _attention(z3, hist_k, hist_v, w["bias_pairs"], w["g_attn"])

    conv0 = jnp.pad(conv_state.astype(F32), ((0, 0), (SUBLANES - (CONV_W - 1), 0), (0, 0)))
    m0b = jnp.broadcast_to(m0.astype(F32)[..., None], m0.shape + (LANES,))
    mlstm, c_new, n_new, m_new = _mlstm(
        z3, gates.reshape(b, s_len, LANES), conv0, c0.astype(F32), n0.astype(F32), m0b,
        w["w_conv"], w["b_conv"], w["gate_bias"], w["g_mlstm"], mlstm_blk, 3)

    bm2 = _largest_divisor(t, 512)
    x1, x1b = _out_proj(attn.reshape(t, d_attn), mlstm.reshape(t, d_b), x2d, w["w_out"],
                        w["ln1_g"], w["ln1_b"], alpha, bm2)
    hmid = _ffn_up(x1b, w["w_ffn_gate"], w["w_ffn_up"], bm, 512)
    y = _ffn_down(hmid, w["w_ffn_down"], x1, w["ln2_g"], w["ln2_b"], alpha, bm, 512)

    n_heads_a = d_attn // HEAD_DIM_A
    k_new = z3[:, :, d_attn:2 * d_attn]
    v_new = z3[:, :, 2 * d_attn:3 * d_attn]
    qk_raw_tail = z3[:, s_len - (CONV_W - 1):, 3 * d_attn:3 * d_attn + 2 * d_b].astype(F32)
    return (y.reshape(b, s_len, d), k_new, v_new, qk_raw_tail, (c_new, n_new, m_new[..., 0]),
            n_heads_a)


def kernel(x_prompt, x_sample, cache_attn_k, cache_attn_v, state_conv, state_mlstm_C, state_mlstm_n, state_mlstm_m, w_in, b_igate, b_fgate, w_conv, b_conv, rel_bias, g_attn_norm, g_mlstm_norm, w_out, ln1_g, ln1_b, w_ffn_gate, w_ffn_up, w_ffn_down, ln2_g, ln2_b):
    depth = w_in.shape[0]
    alpha = (2.0 * depth) ** 0.25
    xp, xs = x_prompt, x_sample
    bp, sp, d_model = xp.shape
    n_heads_b = b_igate.shape[1]
    d_b = n_heads_b * HEAD_DIM_B
    d_attn = g_attn_norm.shape[1]
    n_heads_a = d_attn // HEAD_DIM_A
    n_main = 3 * d_attn + 4 * d_b
    new_p, new_s = [], []
    for l in range(depth):
        w_in_b = w_in[l].astype(BF16)
        gate_bias = jnp.zeros((1, LANES), F32).at[0, :n_heads_b].set(b_igate[l]).at[0, n_heads_b:2 * n_heads_b].set(b_fgate[l])
        w = dict(
            w_main=w_in_b[:, :n_main],
            w_gate=jnp.pad(w_in_b[:, n_main:], ((0, 0), (0, LANES - 2 * n_heads_b))),
            gate_bias=gate_bias,
            w_conv=w_conv[l], b_conv=b_conv[l][None, :],
            bias_pairs=_bias_pairs(rel_bias[l]),
            g_attn=g_attn_norm[l][None, :], g_mlstm=g_mlstm_norm[l][None, :],
            w_out=w_out[l].astype(BF16), ln1_g=ln1_g[l][None, :], ln1_b=ln1_b[l][None, :],
            w_ffn_gate=w_ffn_gate[l].astype(BF16), w_ffn_up=w_ffn_up[l].astype(BF16),
            w_ffn_down=w_ffn_down[l].astype(BF16), ln2_g=ln2_g[l][None, :], ln2_b=ln2_b[l][None, :],
        )
        zero_state = (jnp.zeros((bp, n_heads_b, HEAD_DIM_B, HEAD_DIM_B), F32),
                      jnp.zeros((bp, n_heads_b, HEAD_DIM_B), F32),
                      jnp.zeros((bp, n_heads_b), F32))
        zero_conv = jnp.zeros((bp, CONV_W - 1, 2 * d_b), F32)
        xp, kp, vp, cp, state_p, _ = _trunk_layer(xp, None, None, zero_conv, zero_state, w, alpha,
                                                  _largest_divisor(sp, 128))
        n_keep = min(BAND_ROWS, sp)
        to_heads = lambda a: a.astype(F32).reshape(a.shape[0], a.shape[1], n_heads_a, HEAD_DIM_A)
        new_p.append((to_heads(kp[:, sp - n_keep:]), to_heads(vp[:, sp - n_keep:]), cp) + state_p)
        bs, ls, _ = xs.shape
        ck = cache_attn_k[l].reshape(bs, -1, d_attn)
        cv = cache_attn_v[l].reshape(bs, -1, d_attn)
        xs, ks_, vs, cs, state_s, _ = _trunk_layer(
            xs, ck.astype(BF16), cv.astype(BF16), state_conv[l],
            (state_mlstm_C[l], state_mlstm_n[l], state_mlstm_m[l]), w, alpha, ls)
        n_past = ck.shape[1]
        n_keep_s = min(BAND_ROWS, n_past + ls)
        k_all = jnp.concatenate([ck, ks_.astype(F32)], axis=1)[:, n_past + ls - n_keep_s:]
        v_all = jnp.concatenate([cv, vs.astype(F32)], axis=1)[:, n_past + ls - n_keep_s:]
        new_s.append((to_heads(k_all), to_heads(v_all), cs) + state_s)
    k_p, v_p, conv_p, C_p, n_p, m_p = [jnp.stack(t) for t in zip(*new_p)]
    k_s, v_s, conv_s, C_s, n_s, m_s = [jnp.stack(t) for t in zip(*new_s)]
    return (xp, xs, k_p, v_p, conv_p, C_p, n_p, m_p, k_s, v_s, conv_s, C_s, n_s, m_s)
```

```python
import functools

import jax
import jax.numpy as jnp
from jax import lax
from jax.experimental import pallas as pl
from jax.experimental.pallas import tpu as pltpu

CHUNK = 64
BAND_ROWS = 8 * CHUNK
BAND = BAND_ROWS + CHUNK
REL_CLIP = 256
HEAD_DIM_A = 64
HEAD_DIM_B = 128
CONV_W = 4
LANES = 128
SUBLANES = 8
NEG_BIG = -1e30
VMEM_LIMIT_BYTES = 56 * 1024 * 1024

F32 = jnp.float32
BF16 = jnp.bfloat16


def _params(*semantics, flags=None):
    return pltpu.CompilerParams(dimension_semantics=semantics, vmem_limit_bytes=VMEM_LIMIT_BYTES,
                                flags=flags)


def _mm(a, b):
    return jnp.dot(a, b, preferred_element_type=F32)


def _mm_nt(a, b, precision=None):
    return lax.dot_general(a, b, (((1,), (1,)), ((), ())), precision=precision,
                           preferred_element_type=F32)


def _in_proj_kernel(x_ref, w_ref, wg_ref, z_ref, g_ref, xb_ref):
    @pl.when(pl.program_id(1) == 0)
    def _():
        xb = x_ref[...].astype(BF16)
        xb_ref[...] = xb
        g_ref[...] = _mm(xb, wg_ref[...])

    z_ref[...] = _mm(xb_ref[...], w_ref[...]).astype(z_ref.dtype)


def _in_proj(x2d, w_in, n, w_gate, bm, bn):
    t, d = x2d.shape
    return pl.pallas_call(
        _in_proj_kernel,
        grid=(t // bm, n // bn),
        in_specs=[pl.BlockSpec((bm, d), lambda i, j: (i, 0)),
                  pl.BlockSpec((d, bn), lambda i, j: (0, j)),
                  pl.BlockSpec((d, LANES), lambda i, j: (0, 0))],
        out_specs=[pl.BlockSpec((bm, bn), lambda i, j: (i, j)),
                   pl.BlockSpec((bm, LANES), lambda i, j: (i, 0))],
        out_shape=[jax.ShapeDtypeStruct((t, n), BF16),
                   jax.ShapeDtypeStruct((t, LANES), F32)],
        scratch_shapes=[pltpu.VMEM((bm, d), BF16)],
        compiler_params=_params("parallel", "arbitrary"),
        name="in_proj",
    )(x2d, w_in, w_gate)


def _attn_kernel(*refs, has_hist, n_pairs):
    if has_hist:
        q_ref, k_ref, v_ref, kh_ref, vh_ref, bias_ref, g_ref, o_ref, kpad, vpad = refs
    else:
        q_ref, k_ref, v_ref, bias_ref, g_ref, o_ref, kpad, vpad = refs
    c = pl.program_id(1)
    s_len = k_ref.shape[1]

    @pl.when(c == 0)
    def _():
        if has_hist:
            kpad[0:BAND_ROWS, :] = kh_ref[0]
            vpad[0:BAND_ROWS, :] = vh_ref[0]
        else:
            kpad[0:BAND_ROWS, :] = jnp.zeros((BAND_ROWS, kpad.shape[1]), BF16)
            vpad[0:BAND_ROWS, :] = jnp.zeros((BAND_ROWS, vpad.shape[1]), BF16)
        kpad[BAND_ROWS:BAND_ROWS + s_len, :] = k_ref[0]
        vpad[BAND_ROWS:BAND_ROWS + s_len, :] = v_ref[0]

    if has_hist:
        _attn_chunk(q_ref, kpad, vpad, bias_ref, g_ref, o_ref, c, n_pairs, masked=False)
    else:
        @pl.when(c < BAND_ROWS // CHUNK)
        def _():
            _attn_chunk(q_ref, kpad, vpad, bias_ref, g_ref, o_ref, c, n_pairs, masked=True)

        @pl.when(c >= BAND_ROWS // CHUNK)
        def _():
            _attn_chunk(q_ref, kpad, vpad, bias_ref, g_ref, o_ref, c, n_pairs, masked=False)


def _attn_chunk(q_ref, kpad, vpad, bias_ref, g_ref, o_ref, c, n_pairs, masked):
    start = pl.multiple_of(c * CHUNK, CHUNK)
    lane = lax.broadcasted_iota(jnp.int32, (CHUNK, LANES), 1)
    row2 = lax.broadcasted_iota(jnp.int32, (2 * CHUNK, LANES), 0)
    lane2 = lax.broadcasted_iota(jnp.int32, (2 * CHUNK, LANES), 1)
    own_head = (row2 >= CHUNK) == (lane2 >= HEAD_DIM_A)
    if masked:
        col = lax.broadcasted_iota(jnp.int32, (2 * CHUNK, BAND), 1)
        valid = col >= BAND_ROWS - c * CHUNK

    def scores(p):
        cols = slice(p * LANES, (p + 1) * LANES)
        qp = q_ref[0, :, cols] * (HEAD_DIM_A ** -0.5)
        zero = jnp.zeros_like(qp)
        q2 = jnp.concatenate([jnp.where(lane < HEAD_DIM_A, qp, zero),
                              jnp.where(lane >= HEAD_DIM_A, qp, zero)], axis=0)
        return _mm_nt(q2, kpad[pl.ds(start, BAND), cols])

    pairs = range(n_pairs)
    s = [scores(p) + bias_ref[p] for p in pairs]
    if masked:
        s = [jnp.where(valid, s[p], NEG_BIG) for p in pairs]
    m = [jnp.max(s[p], axis=-1, keepdims=True) for p in pairs]
    e = [jnp.exp(s[p] - m[p]) for p in pairs]
    denom = [jnp.sum(e[p], axis=-1, keepdims=True) for p in pairs]
    o = [_mm(e[p].astype(BF16), vpad[pl.ds(start, BAND), p * LANES:(p + 1) * LANES]) for p in pairs]
    o = [jnp.where(own_head, o[p] / denom[p], 0.0) for p in pairs]
    ssq = [jnp.sum(o[p] * o[p], axis=-1, keepdims=True) for p in pairs]
    for p in pairs:
        cols = slice(p * LANES, (p + 1) * LANES)
        on = o[p] * lax.rsqrt(ssq[p] * (1.0 / HEAD_DIM_A) + 1e-6)
        o_ref[0, :, cols] = ((on[:CHUNK] + on[CHUNK:]) * g_ref[:, cols]).astype(o_ref.dtype)


def _attention(z3, hist_k, hist_v, bias_pairs, g_attn):
    b, s_len, _ = z3.shape
    d_attn = g_attn.shape[1]
    n_pairs = d_attn // LANES
    has_hist = hist_k is not None
    in_specs = [pl.BlockSpec((1, CHUNK, d_attn), lambda i, c: (i, c, 0)),
                pl.BlockSpec((1, s_len, d_attn), lambda i, c: (i, 0, 1)),
                pl.BlockSpec((1, s_len, d_attn), lambda i, c: (i, 0, 2))]
    args = [z3, z3, z3]
    if has_hist:
        in_specs += [pl.BlockSpec((1, BAND_ROWS, d_attn), lambda i, c: (i, 0, 0))] * 2
        args += [hist_k, hist_v]
    in_specs += [pl.BlockSpec(bias_pairs.shape, lambda i, c: (0, 0, 0)),
                 pl.BlockSpec((1, d_attn), lambda i, c: (0, 0))]
    args += [bias_pairs, g_attn]
    return pl.pallas_call(
        functools.partial(_attn_kernel, has_hist=has_hist, n_pairs=n_pairs),
        grid=(b, s_len // CHUNK),
        in_specs=in_specs,
        out_specs=pl.BlockSpec((1, CHUNK, d_attn), lambda i, c: (i, c, 0)),
        out_shape=jax.ShapeDtypeStruct((b, s_len, d_attn), BF16),
        scratch_shapes=[pltpu.VMEM((BAND_ROWS + s_len, d_attn), BF16)] * 2,
        compiler_params=_params("parallel", "arbitrary"),
        name="attn_hist" if has_hist else "attn",
    )(*args)


def _bias_pairs(rel_bias):
    n_heads = rel_bias.shape[0]
    n_far = BAND - REL_CLIP
    far = jnp.broadcast_to(rel_bias[:, 2 * REL_CLIP:], (n_heads, n_far))
    near = rel_bias[:, REL_CLIP - (CHUNK - 1):2 * REL_CLIP][:, ::-1]
    ext = jnp.concatenate([far, near], axis=1)
    bias = jnp.stack([ext[:, CHUNK - 1 - i:CHUNK - 1 - i + BAND] for i in range(CHUNK)], axis=1)
    return bias.reshape(n_heads // 2, 2 * CHUNK, BAND).astype(F32)


def _log_sigmoid(x):
    return jnp.minimum(x, 0.0) - jnp.log1p(jnp.exp(-jnp.abs(x)))


def _mlstm_kernel(q_ref, k_ref, v_ref, og_ref, gates_ref, conv0_ref, c0_ref, n0_ref, m0_ref,
                  wconv_ref, bconv_ref, gbias_ref, gnorm_ref,
                  o_ref, c_out_ref, n_out_ref, m_out_ref,
                  xs, c_s, n_s, m_s, *, n_heads, blk):
    c = pl.program_id(1)
    last = pl.num_programs(1) - 1
    d_b = n_heads * HEAD_DIM_B
    pad = SUBLANES

    @pl.when(c == 0)
    def _():
        xs[0:pad, :] = conv0_ref[0]
        c_s[...] = c0_ref[0]
        n_s[...] = n0_ref[0]
        m_s[...] = m0_ref[0]

    xs[pad:pad + blk, 0:d_b] = q_ref[0].astype(F32)
    xs[pad:pad + blk, d_b:2 * d_b] = k_ref[0].astype(F32)

    gates = gates_ref[0] + gbias_ref[...]
    lane = lax.broadcasted_iota(jnp.int32, (blk, LANES), 1)
    row_i = lax.broadcasted_iota(jnp.int32, (blk, blk), 0)
    col_i = lax.broadcasted_iota(jnp.int32, (blk, blk), 1)
    causal = row_i >= col_i
    tril = jnp.where(causal, 1.0, 0.0).astype(F32)
    f_cum = jnp.dot(tril, _log_sigmoid(gates), precision=lax.Precision.HIGHEST,
                    preferred_element_type=F32)
    sel_r = lax.broadcasted_iota(jnp.int32, (SUBLANES, LANES), 0)
    sel_c = lax.broadcasted_iota(jnp.int32, (SUBLANES, LANES), 1)
    sel = (jnp.where(sel_c == sel_r, 1.0, 0.0) - jnp.where(sel_c == sel_r + n_heads, 1.0, 0.0)).astype(F32)
    ig_minus_f_rows = _mm_nt(sel, jnp.where(lane < n_heads, gates, f_cum),
                             precision=lax.Precision.HIGHEST)

    def conv_silu(col0):
        acc = bconv_ref[:, col0:col0 + HEAD_DIM_B]
        for j in range(CONV_W):
            acc = acc + xs[pl.ds(pad - (CONV_W - 1) + j, blk), col0:col0 + HEAD_DIM_B] * wconv_ref[j:j + 1, col0:col0 + HEAD_DIM_B]
        return acc * jax.nn.sigmoid(acc)

    for h in range(n_heads):
        cols = slice(h * HEAD_DIM_B, (h + 1) * HEAD_DIM_B)
        q = conv_silu(h * HEAD_DIM_B)
        k = conv_silu(d_b + h * HEAD_DIM_B) * (HEAD_DIM_B ** -0.5)
        v = v_ref[0, :, cols]
        qb = q.astype(BF16)
        kb = k.astype(BF16)

        ig = gates[:, h:h + 1]
        f_h = f_cum[:, n_heads + h:n_heads + h + 1]
        m_prev = m_s[h:h + 1, 0:1]
        a = f_h + m_prev
        d = jnp.where(causal, f_h + ig_minus_f_rows[h:h + 1, :], NEG_BIG)
        m_t = jnp.maximum(a, jnp.max(d, axis=-1, keepdims=True))
        w_inter = jnp.exp(a - m_t)
        s = _mm_nt(qb, kb) * jnp.exp(d - m_t)
        c_prev = c_s[h]
        n_prev = n_s[h:h + 1, :]
        num = _mm(s.astype(BF16), v) + w_inter * _mm(qb, c_prev.astype(BF16))
        den = jnp.sum(s, axis=-1, keepdims=True) + w_inter * jnp.sum(q * n_prev, axis=-1, keepdims=True)
        hid = num / jnp.maximum(jnp.abs(den), jnp.exp(-m_t))

        f_last = f_h[blk - 1:blk, :]
        g = f_last - f_h + ig
        a_last = f_last + m_prev
        m_new = jnp.maximum(a_last, jnp.max(g, axis=0, keepdims=True))
        w_old = jnp.exp(a_last - m_new)
        kw = k * jnp.exp(g - m_new)
        c_new = w_old * c_prev + lax.dot_general(kw.astype(BF16), v, (((0,), (0,)), ((), ())),
                                                 preferred_element_type=F32)
        n_new = w_old * n_prev + jnp.sum(kw, axis=0, keepdims=True)
        c_s[h] = c_new
        n_s[h:h + 1, :] = n_new
        m_s[h:h + 1, :] = jnp.broadcast_to(m_new, (1, LANES))

        hn = hid * lax.rsqrt(jnp.mean(hid * hid, axis=-1, keepdims=True) + 1e-6) * gnorm_ref[:, cols]
        o_ref[0, :, cols] = (hn * jax.nn.sigmoid(og_ref[0, :, cols].astype(F32))).astype(o_ref.dtype)

    xs[0:pad, :] = xs[blk:blk + pad, :]

    @pl.when(c == last)
    def _():
        c_out_ref[0] = c_s[...]
        n_out_ref[0] = n_s[...]
        m_out_ref[0] = m_s[...]


def _mlstm(z3, gates3, conv0, c0, n0, m0, w_conv, b_conv, gate_bias, g_norm, blk, col_block0):
    b, s_len, _ = z3.shape
    n_heads = c0.shape[1]
    d_b = n_heads * HEAD_DIM_B
    zspec = lambda k: pl.BlockSpec((1, blk, d_b), lambda i, c: (i, c, col_block0 + k))
    full = lambda a: pl.BlockSpec(a.shape, lambda i, c: (0,) * a.ndim)
    per_b = lambda a: pl.BlockSpec((1,) + a.shape[1:], lambda i, c: (i,) + (0,) * (a.ndim - 1))
    return pl.pallas_call(
        functools.partial(_mlstm_kernel, n_heads=n_heads, blk=blk),
        grid=(b, s_len // blk),
        in_specs=[zspec(0), zspec(1), zspec(2), zspec(3),
                  pl.BlockSpec((1, blk, LANES), lambda i, c: (i, c, 0)),
                  per_b(conv0), per_b(c0), per_b(n0), per_b(m0),
                  full(w_conv), full(b_conv), full(gate_bias), full(g_norm)],
        out_specs=[pl.BlockSpec((1, blk, d_b), lambda i, c: (i, c, 0)),
                   per_b(c0), per_b(n0), per_b(m0)],
        out_shape=[jax.ShapeDtypeStruct((b, s_len, d_b), BF16),
                   jax.ShapeDtypeStruct(c0.shape, F32),
                   jax.ShapeDtypeStruct(n0.shape, F32),
                   jax.ShapeDtypeStruct(m0.shape, F32)],
        scratch_shapes=[pltpu.VMEM((blk + SUBLANES, 2 * d_b), F32),
                        pltpu.VMEM(c0.shape[1:], F32),
                        pltpu.VMEM(n0.shape[1:], F32),
                        pltpu.VMEM(m0.shape[1:], F32)],
        compiler_params=_params("parallel", "arbitrary"),
        name="mlstm",
    )(z3, z3, z3, z3, gates3, conv0, c0, n0, m0, w_conv, b_conv, gate_bias, g_norm)


def _layer_norm(r, g, b):
    mu = jnp.mean(r, axis=-1, keepdims=True)
    rc = r - mu
    var = jnp.mean(rc * rc, axis=-1, keepdims=True)
    return rc * lax.rsqrt(var + 1e-5) * g + b


def _out_proj_kernel(a_ref, m_ref, x_ref, wa_ref, wm_ref, g_ref, b_ref, x1_ref, x1b_ref, *, alpha):
    mix = _mm(a_ref[...], wa_ref[...]) + _mm(m_ref[...], wm_ref[...])
    y = _layer_norm(alpha * x_ref[...] + mix, g_ref[...], b_ref[...])
    x1_ref[...] = y
    x1b_ref[...] = y.astype(BF16)


def _out_proj(attn2d, mlstm2d, x2d, w_out, ln_g, ln_b, alpha, bm):
    t, d = x2d.shape
    da, db = attn2d.shape[1], mlstm2d.shape[1]
    assert da == db
    row = lambda w: pl.BlockSpec((bm, w), lambda i: (i, 0))
    vec = pl.BlockSpec((1, d), lambda i: (0, 0))
    return pl.pallas_call(
        functools.partial(_out_proj_kernel, alpha=alpha),
        grid=(t // bm,),
        in_specs=[row(da), row(db), row(d),
                  pl.BlockSpec((da, d), lambda i: (0, 0)),
                  pl.BlockSpec((db, d), lambda i: (1, 0)),
                  vec, vec],
        out_specs=[row(d), row(d)],
        out_shape=[jax.ShapeDtypeStruct((t, d), F32), jax.ShapeDtypeStruct((t, d), BF16)],
        compiler_params=_params("parallel"),
        name="out_proj_ln",
    )(attn2d, mlstm2d, x2d, w_out, w_out, ln_g, ln_b)


def _ffn_up_kernel(x_ref, wg_ref, wu_ref, h_ref):
    x = x_ref[...]
    gate = _mm(x, wg_ref[...])
    h_ref[...] = (gate * jax.nn.sigmoid(gate) * _mm(x, wu_ref[...])).astype(h_ref.dtype)


def _ffn_up(x1b, w_gate, w_up, bm, bf):
    t, d = x1b.shape
    f = w_gate.shape[1]
    wspec = pl.BlockSpec((d, bf), lambda i, j: (0, j))
    return pl.pallas_call(
        _ffn_up_kernel,
        grid=(t // bm, f // bf),
        in_specs=[pl.BlockSpec((bm, d), lambda i, j: (i, 0)), wspec, wspec],
        out_specs=pl.BlockSpec((bm, bf), lambda i, j: (i, j)),
        out_shape=jax.ShapeDtypeStruct((t, f), BF16),
        compiler_params=_params("parallel", "arbitrary"),
        name="ffn_up",
    )(x1b, w_gate, w_up)


def _ffn_down_kernel(h_ref, wd_ref, x1_ref, g_ref, b_ref, y_ref, *, alpha):
    y_ref[...] = _layer_norm(alpha * x1_ref[...] + _mm(h_ref[...], wd_ref[...]), g_ref[...], b_ref[...])


def _ffn_down(h, w_down, x1, ln_g, ln_b, alpha, bm):
    t, f = h.shape
    d = w_down.shape[1]
    vec = pl.BlockSpec((1, d), lambda i: (0, 0))
    return pl.pallas_call(
        functools.partial(_ffn_down_kernel, alpha=alpha),
        grid=(t // bm,),
        in_specs=[pl.BlockSpec((bm, f), lambda i: (i, 0)),
                  pl.BlockSpec((f, d), lambda i: (0, 0), pipeline_mode=pl.Buffered(1)),
                  pl.BlockSpec((bm, d), lambda i: (i, 0)),
                  vec, vec],
        out_specs=pl.BlockSpec((bm, d), lambda i: (i, 0)),
        out_shape=jax.ShapeDtypeStruct((t, d), F32),
        compiler_params=_params("parallel"),
        name="ffn_down_ln",
    )(h, w_down, x1, ln_g, ln_b)


def _largest_divisor(n, cap):
    return max(d for d in range(1, min(n, cap) + 1) if n % d == 0)


def _trunk_layer(x, hist_k, hist_v, conv_state, mlstm_state, w, alpha, mlstm_blk):
    b, s_len, d = x.shape
    t = b * s_len
    c0, n0, m0 = mlstm_state
    n_heads_b = c0.shape[1]
    d_b = n_heads_b * HEAD_DIM_B
    d_attn = w["g_attn"].shape[1]
    n_main = 3 * d_attn + 4 * d_b
    assert d_attn == d_b and w["w_in"].shape[1] == n_main + 2 * n_heads_b

    bm = _largest_divisor(t, 1024)
    x2d = x.reshape(t, d)
    z, gates = _in_proj(x2d, w["w_in"], n_main, w["w_gate"], bm, 1024)
    z3 = z.reshape(b, s_len, -1)

    attn = _attention(z3, hist_k, hist_v, w["bias_pairs"], w["g_attn"])

    conv0 = jnp.pad(conv_state.astype(F32), ((0, 0), (SUBLANES - (CONV_W - 1), 0), (0, 0)))
    m0b = jnp.broadcast_to(m0.astype(F32)[..., None], m0.shape + (LANES,))
    mlstm, c_new, n_new, m_new = _mlstm(
        z3, gates.reshape(b, s_len, LANES), conv0, c0.astype(F32), n0.astype(F32), m0b,
        w["w_conv"], w["b_conv"], w["gate_bias"], w["g_mlstm"], mlstm_blk, 3)

    bm2 = _largest_divisor(t, 512)
    x1, x1b = _out_proj(attn.reshape(t, d_attn), mlstm.reshape(t, d_b), x2d, w["w_out"],
                        w["ln1_g"], w["ln1_b"], alpha, bm2)
    hmid = _ffn_up(x1b, w["w_ffn_gate"], w["w_ffn_up"], bm, 512)
    y = _ffn_down(hmid, w["w_ffn_down"], x1, w["ln2_g"], w["ln2_b"], alpha, bm2)

    n_heads_a = d_attn // HEAD_DIM_A
    k_new = z3[:, :, d_attn:2 * d_attn]
    v_new = z3[:, :, 2 * d_attn:3 * d_attn]
    qk_raw_tail = z3[:, s_len - (CONV_W - 1):, 3 * d_attn:3 * d_attn + 2 * d_b].astype(F32)
    return (y.reshape(b, s_len, d), k_new, v_new, qk_raw_tail, (c_new, n_new, m_new[..., 0]),
            n_heads_a)


def kernel(x_prompt, x_sample, cache_attn_k, cache_attn_v, state_conv, state_mlstm_C, state_mlstm_n, state_mlstm_m, w_in, b_igate, b_fgate, w_conv, b_conv, rel_bias, g_attn_norm, g_mlstm_norm, w_out, ln1_g, ln1_b, w_ffn_gate, w_ffn_up, w_ffn_down, ln2_g, ln2_b):
    depth = w_in.shape[0]
    alpha = (2.0 * depth) ** 0.25
    xp, xs = x_prompt, x_sample
    bp, sp, d_model = xp.shape
    n_heads_b = b_igate.shape[1]
    d_b = n_heads_b * HEAD_DIM_B
    d_attn = g_attn_norm.shape[1]
    n_heads_a = d_attn // HEAD_DIM_A
    n_main = 3 * d_attn + 4 * d_b
    new_p, new_s = [], []
    for l in range(depth):
        w_in_b = w_in[l].astype(BF16)
        gate_bias = jnp.zeros((1, LANES), F32).at[0, :n_heads_b].set(b_igate[l]).at[0, n_heads_b:2 * n_heads_b].set(b_fgate[l])
        w = dict(
            w_in=w_in_b,
            w_gate=jnp.pad(w_in_b[:, n_main:], ((0, 0), (0, LANES - 2 * n_heads_b))),
            gate_bias=gate_bias,
            w_conv=w_conv[l], b_conv=b_conv[l][None, :],
            bias_pairs=_bias_pairs(rel_bias[l]),
            g_attn=g_attn_norm[l][None, :], g_mlstm=g_mlstm_norm[l][None, :],
            w_out=w_out[l].astype(BF16), ln1_g=ln1_g[l][None, :], ln1_b=ln1_b[l][None, :],
            w_ffn_gate=w_ffn_gate[l].astype(BF16), w_ffn_up=w_ffn_up[l].astype(BF16),
            w_ffn_down=w_ffn_down[l].astype(BF16), ln2_g=ln2_g[l][None, :], ln2_b=ln2_b[l][None, :],
        )
        zero_state = (jnp.zeros((bp, n_heads_b, HEAD_DIM_B, HEAD_DIM_B), F32),
                      jnp.zeros((bp, n_heads_b, HEAD_DIM_B), F32),
                      jnp.zeros((bp, n_heads_b), F32))
        zero_conv = jnp.zeros((bp, CONV_W - 1, 2 * d_b), F32)
        xp, kp, vp, cp, state_p, _ = _trunk_layer(xp, None, None, zero_conv, zero_state, w, alpha,
                                                  _largest_divisor(sp, 128))
        n_keep = min(BAND_ROWS, sp)
        to_heads = lambda a: a.astype(F32).reshape(a.shape[0], a.shape[1], n_heads_a, HEAD_DIM_A)
        new_p.append((to_heads(kp[:, sp - n_keep:]), to_heads(vp[:, sp - n_keep:]), cp) + state_p)
        bs, ls, _ = xs.shape
        ck = cache_attn_k[l].reshape(bs, -1, d_attn)
        cv = cache_attn_v[l].reshape(bs, -1, d_attn)
        xs, ks_, vs, cs, state_s, _ = _trunk_layer(
            xs, ck.astype(BF16), cv.astype(BF16), state_conv[l],
            (state_mlstm_C[l], state_mlstm_n[l], state_mlstm_m[l]), w, alpha, ls)
        n_past = ck.shape[1]
        n_keep_s = min(BAND_ROWS, n_past + ls)
        k_all = jnp.concatenate([ck, ks_.astype(F32)], axis=1)[:, n_past + ls - n_keep_s:]
        v_all = jnp.concatenate([cv, vs.astype(F32)], axis=1)[:, n_past + ls - n_keep_s:]
        new_s.append((to_heads(k_all), to_heads(v_all), cs) + state_s)
    k_p, v_p, conv_p, C_p, n_p, m_p = [jnp.stack(t) for t in zip(*new_p)]
    k_s, v_s, conv_s, C_s, n_s, m_s = [jnp.stack(t) for t in zip(*new_s)]
    return (xp, xs, k_p, v_p, conv_p, C_p, n_p, m_p, k_s, v_s, conv_s, C_s, n_s, m_s)
```

```python
import functools

import jax
import jax.numpy as jnp
from jax import lax
from jax.experimental import pallas as pl
from jax.experimental.pallas import tpu as pltpu

CHUNK = 64
BAND_ROWS = 8 * CHUNK
BAND = BAND_ROWS + CHUNK
REL_CLIP = 256
HEAD_DIM_A = 64
HEAD_DIM_B = 128
CONV_W = 4
LANES = 128
SUBLANES = 8
NEG_BIG = -1e30
VMEM_LIMIT_BYTES = 56 * 1024 * 1024

F32 = jnp.float32
BF16 = jnp.bfloat16


def _params(*semantics, flags=None):
    return pltpu.CompilerParams(dimension_semantics=semantics, vmem_limit_bytes=VMEM_LIMIT_BYTES,
                                flags=flags)


def _mm(a, b):
    return jnp.dot(a, b, preferred_element_type=F32)


def _mm_nt(a, b, precision=None):
    return lax.dot_general(a, b, (((1,), (1,)), ((), ())), precision=precision,
                           preferred_element_type=F32)


def _in_proj_kernel(x_ref, w_ref, wg_ref, z_ref, g_ref, xb_ref):
    @pl.when(pl.program_id(1) == 0)
    def _():
        xb = x_ref[...].astype(BF16)
        xb_ref[...] = xb
        g_ref[...] = _mm(xb, wg_ref[...])

    z_ref[...] = _mm(xb_ref[...], w_ref[...]).astype(z_ref.dtype)


def _in_proj(x2d, w_in, n, w_gate, bm, bn):
    t, d = x2d.shape
    ng = w_gate.shape[1]
    return pl.pallas_call(
        _in_proj_kernel,
        grid=(t // bm, n // bn),
        in_specs=[pl.BlockSpec((bm, d), lambda i, j: (i, 0)),
                  pl.BlockSpec((d, bn), lambda i, j: (0, j)),
                  pl.BlockSpec((d, ng), lambda i, j: (0, 0))],
        out_specs=[pl.BlockSpec((bm, bn), lambda i, j: (i, j)),
                   pl.BlockSpec((bm, ng), lambda i, j: (i, 0))],
        out_shape=[jax.ShapeDtypeStruct((t, n), BF16),
                   jax.ShapeDtypeStruct((t, ng), F32)],
        scratch_shapes=[pltpu.VMEM((bm, d), BF16)],
        compiler_params=_params("parallel", "arbitrary"),
        name="in_proj",
    )(x2d, w_in, w_gate)


def _attn_kernel(*refs, has_hist, n_pairs):
    if has_hist:
        q_ref, k_ref, v_ref, kh_ref, vh_ref, bias_ref, g_ref, o_ref, kpad, vpad = refs
    else:
        q_ref, k_ref, v_ref, bias_ref, g_ref, o_ref, kpad, vpad = refs
    c = pl.program_id(1)
    s_len = k_ref.shape[1]

    @pl.when(c == 0)
    def _():
        if has_hist:
            kpad[0:BAND_ROWS, :] = kh_ref[0]
            vpad[0:BAND_ROWS, :] = vh_ref[0]
        else:
            kpad[0:BAND_ROWS, :] = jnp.zeros((BAND_ROWS, kpad.shape[1]), BF16)
            vpad[0:BAND_ROWS, :] = jnp.zeros((BAND_ROWS, vpad.shape[1]), BF16)
        kpad[BAND_ROWS:BAND_ROWS + s_len, :] = k_ref[0]
        vpad[BAND_ROWS:BAND_ROWS + s_len, :] = v_ref[0]

    if has_hist:
        _attn_chunk(q_ref, kpad, vpad, bias_ref, g_ref, o_ref, c, n_pairs, masked=False)
    else:
        @pl.when(c < BAND_ROWS // CHUNK)
        def _():
            _attn_chunk(q_ref, kpad, vpad, bias_ref, g_ref, o_ref, c, n_pairs, masked=True)

        @pl.when(c >= BAND_ROWS // CHUNK)
        def _():
            _attn_chunk(q_ref, kpad, vpad, bias_ref, g_ref, o_ref, c, n_pairs, masked=False)


def _attn_chunk(q_ref, kpad, vpad, bias_ref, g_ref, o_ref, c, n_pairs, masked):
    start = pl.multiple_of(c * CHUNK, CHUNK)
    lane = lax.broadcasted_iota(jnp.int32, (CHUNK, LANES), 1)
    row2 = lax.broadcasted_iota(jnp.int32, (2 * CHUNK, LANES), 0)
    lane2 = lax.broadcasted_iota(jnp.int32, (2 * CHUNK, LANES), 1)
    own_head = (row2 >= CHUNK) == (lane2 >= HEAD_DIM_A)
    if masked:
        col = lax.broadcasted_iota(jnp.int32, (2 * CHUNK, BAND), 1)
        valid = col >= BAND_ROWS - c * CHUNK

    def scores(p):
        cols = slice(p * LANES, (p + 1) * LANES)
        qp = q_ref[0, :, cols] * (HEAD_DIM_A ** -0.5)
        zero = jnp.zeros_like(qp)
        q2 = jnp.concatenate([jnp.where(lane < HEAD_DIM_A, qp, zero),
                              jnp.where(lane >= HEAD_DIM_A, qp, zero)], axis=0)
        return _mm_nt(q2, kpad[pl.ds(start, BAND), cols])

    pairs = range(n_pairs)
    s = [scores(p) + bias_ref[p] for p in pairs]
    if masked:
        s = [jnp.where(valid, s[p], NEG_BIG) for p in pairs]
    m = [jnp.max(s[p], axis=-1, keepdims=True) for p in pairs]
    e = [jnp.exp(s[p] - m[p]) for p in pairs]
    denom = [jnp.sum(e[p], axis=-1, keepdims=True) for p in pairs]
    o = [_mm(e[p].astype(BF16), vpad[pl.ds(start, BAND), p * LANES:(p + 1) * LANES]) for p in pairs]
    o = [jnp.where(own_head, o[p] / denom[p], 0.0) for p in pairs]
    ssq = [jnp.sum(o[p] * o[p], axis=-1, keepdims=True) for p in pairs]
    for p in pairs:
        cols = slice(p * LANES, (p + 1) * LANES)
        on = o[p] * lax.rsqrt(ssq[p] * (1.0 / HEAD_DIM_A) + 1e-6)
        o_ref[0, :, cols] = ((on[:CHUNK] + on[CHUNK:]) * g_ref[:, cols]).astype(o_ref.dtype)


def _attention(z3, hist_k, hist_v, bias_pairs, g_attn):
    b, s_len, _ = z3.shape
    d_attn = g_attn.shape[1]
    n_pairs = d_attn // LANES
    has_hist = hist_k is not None
    in_specs = [pl.BlockSpec((1, CHUNK, d_attn), lambda i, c: (i, c, 0)),
                pl.BlockSpec((1, s_len, d_attn), lambda i, c: (i, 0, 1)),
                pl.BlockSpec((1, s_len, d_attn), lambda i, c: (i, 0, 2))]
    args = [z3, z3, z3]
    if has_hist:
        in_specs += [pl.BlockSpec((1, BAND_ROWS, d_attn), lambda i, c: (i, 0, 0))] * 2
        args += [hist_k, hist_v]
    in_specs += [pl.BlockSpec(bias_pairs.shape, lambda i, c: (0, 0, 0)),
                 pl.BlockSpec((1, d_attn), lambda i, c: (0, 0))]
    args += [bias_pairs, g_attn]
    return pl.pallas_call(
        functools.partial(_attn_kernel, has_hist=has_hist, n_pairs=n_pairs),
        grid=(b, s_len // CHUNK),
        in_specs=in_specs,
        out_specs=pl.BlockSpec((1, CHUNK, d_attn), lambda i, c: (i, c, 0)),
        out_shape=jax.ShapeDtypeStruct((b, s_len, d_attn), BF16),
        scratch_shapes=[pltpu.VMEM((BAND_ROWS + s_len, d_attn), BF16)] * 2,
        compiler_params=_params("parallel", "arbitrary"),
        name="attn_hist" if has_hist else "attn",
    )(*args)


def _bias_pairs(rel_bias):
    n_heads = rel_bias.shape[0]
    n_far = BAND - REL_CLIP
    far = jnp.broadcast_to(rel_bias[:, 2 * REL_CLIP:], (n_heads, n_far))
    near = rel_bias[:, REL_CLIP - (CHUNK - 1):2 * REL_CLIP][:, ::-1]
    ext = jnp.concatenate([far, near], axis=1)
    bias = jnp.stack([ext[:, CHUNK - 1 - i:CHUNK - 1 - i + BAND] for i in range(CHUNK)], axis=1)
    return bias.reshape(n_heads // 2, 2 * CHUNK, BAND).astype(F32)


MLSTM_BLK = 128
STATE_ROWS = HEAD_DIM_B + 16


def _log_sigmoid(x):
    return jnp.minimum(x, 0.0) - jnp.log1p(jnp.exp(-jnp.abs(x)))


def _mm_tn(a, b):
    return lax.dot_general(a, b, (((0,), (0,)), ((), ())), preferred_element_type=F32)


def _mlstm_kernel(q_ref, k_ref, v_ref, og_ref, gates_ref, conv0_ref, c0_ref, n0_ref, m0_ref,
                  wconv_ref, bconv_ref, gbias_ref, gnorm_ref,
                  o_ref, c_out_ref, n_out_ref, m_out_ref,
                  xs, qb_s, k_s, ct_s, m_s, *, n_heads, valid_len):
    blk = MLSTM_BLK
    hd = HEAD_DIM_B
    c = pl.program_id(1)
    last = pl.num_programs(1) - 1
    d_b = n_heads * hd
    pad = SUBLANES
    heads = range(n_heads)

    @pl.when(c == 0)
    def _():
        xs[0:pad, :] = conv0_ref[0]
        for h in heads:
            ct_s[h, 0:hd, :] = c0_ref[0, h].T
            ct_s[h, hd:, :] = jnp.zeros((STATE_ROWS - hd, hd), F32)
            ct_s[h, hd:hd + 1, :] = n0_ref[0, h:h + 1, :]
        m_s[...] = m0_ref[0]

    xs[pad:pad + blk, 0:d_b] = q_ref[0].astype(F32)
    xs[pad:pad + blk, d_b:2 * d_b] = k_ref[0].astype(F32)

    gates = gates_ref[0] + gbias_ref[...]
    ig = gates[:, :LANES]
    lf = _log_sigmoid(gates[:, LANES:])
    if valid_len < blk:
        live = lax.broadcasted_iota(jnp.int32, (blk, LANES), 0) < valid_len
        ig = jnp.where(live, ig, NEG_BIG)
        lf = jnp.where(live, lf, 0.0)
    row_i = lax.broadcasted_iota(jnp.int32, (blk, blk), 0)
    col_i = lax.broadcasted_iota(jnp.int32, (blk, blk), 1)
    f_cum = jnp.dot(jnp.where(row_i >= col_i, 1.0, 0.0).astype(F32), lf,
                    precision=lax.Precision.HIGHEST, preferred_element_type=F32)
    sel_rows = -(-n_heads // SUBLANES) * SUBLANES
    sel = jnp.where(lax.broadcasted_iota(jnp.int32, (sel_rows, LANES), 0)
                    == lax.broadcasted_iota(jnp.int32, (sel_rows, LANES), 1), 1.0, 0.0).astype(F32)
    f_rows = _mm_nt(sel, f_cum, precision=lax.Precision.HIGHEST)
    imf = ig - f_cum
    m_prev = m_s[...]
    f_last = f_cum[blk - 1:blk, :]
    g = f_last + imf
    a_last = f_last + m_prev
    m_new = jnp.maximum(a_last, jnp.max(g, axis=0, keepdims=True))
    w_old = jnp.exp(a_last - m_new)
    wk = jnp.exp(g - m_new)

    for j in range(2 * n_heads):
        cols = slice(j * hd, (j + 1) * hd)
        acc = bconv_ref[:, cols]
        for tap in range(CONV_W):
            acc = acc + xs[pl.ds(pad - (CONV_W - 1) + tap, blk), cols] * wconv_ref[tap:tap + 1, cols]
        y = acc * jax.nn.sigmoid(acc)
        if j < n_heads:
            qb_s[:, cols] = y.astype(BF16)
        else:
            k_s[:, (j - n_heads) * hd:(j - n_heads + 1) * hd] = y * (hd ** -0.5)

    hcols = [slice(h * hd, (h + 1) * hd) for h in heads]
    st = [_mm_nt(k_s[:, hcols[h]].astype(BF16), qb_s[:, hcols[h]]) for h in heads]
    inter = [_mm_nt(ct_s[h].astype(BF16), qb_s[:, hcols[h]]) for h in heads]

    visible = row_i <= col_i
    f_row = [f_rows[h:h + 1, :] for h in heads]
    a_row = [f_row[h] + m_prev[:, h:h + 1] for h in heads]
    dt = [jnp.where(visible, jnp.broadcast_to(imf[:, h:h + 1], (blk, blk)) + f_row[h], NEG_BIG) for h in heads]
    m_t = [jnp.maximum(a_row[h], jnp.max(dt[h], axis=0, keepdims=True)) for h in heads]
    pt = [st[h] * jnp.exp(dt[h] - m_t[h]) for h in heads]
    w_inter = [jnp.exp(a_row[h] - m_t[h]) for h in heads]
    den = [jnp.sum(pt[h], axis=0, keepdims=True) + w_inter[h] * inter[h][hd:hd + 1, :] for h in heads]
    scale = [1.0 / jnp.maximum(jnp.abs(den[h]), jnp.exp(-m_t[h])) for h in heads]
    kw = [k_s[:, hcols[h]] * jnp.broadcast_to(wk[:, h:h + 1], (blk, hd)) for h in heads]

    num_t = [_mm_tn(v_ref[0, :, hcols[h]], pt[h].astype(BF16)) for h in heads]
    dct = [_mm_tn(v_ref[0, :, hcols[h]], kw[h].astype(BF16)) for h in heads]

    ht = [(num_t[h] + w_inter[h] * inter[h][0:hd, :]) * scale[h] for h in heads]
    ssq = [jnp.sum(ht[h] * ht[h], axis=0, keepdims=True) for h in heads]
    for h in heads:
        hn = (ht[h] * lax.rsqrt(ssq[h] * (1.0 / hd) + 1e-6)).T
        out = hn * gnorm_ref[:, hcols[h]] * jax.nn.sigmoid(og_ref[0, :, hcols[h]].astype(F32))
        o_ref[0, :, hcols[h]] = out.astype(o_ref.dtype)
        decay = w_old[:, h:h + 1]
        ct_s[h, 0:hd, :] = decay * ct_s[h, 0:hd, :] + dct[h]
        ct_s[h, hd:hd + 1, :] = decay * ct_s[h, hd:hd + 1, :] + jnp.sum(kw[h], axis=0, keepdims=True)
    m_s[...] = m_new
    xs[0:pad, :] = xs[blk:blk + pad, :]

    @pl.when(c == last)
    def _():
        for h in heads:
            c_out_ref[0, h] = ct_s[h, 0:hd, :].T
            n_out_ref[0, h:h + 1, :] = ct_s[h, hd:hd + 1, :]
        m_out_ref[0] = m_s[...]


def _mlstm(z3, gates3, conv0, c0, n0, m0, w_conv, b_conv, gate_bias, g_norm, col_block0, valid_len):
    b, s_len, _ = z3.shape
    n_heads = c0.shape[1]
    d_b = n_heads * HEAD_DIM_B
    blk = MLSTM_BLK
    assert s_len % blk == 0 and n_heads <= LANES and (valid_len == blk or s_len == blk)
    zspec = lambda k: pl.BlockSpec((1, blk, d_b), lambda i, c: (i, c, col_block0 + k))
    full = lambda a: pl.BlockSpec(a.shape, lambda i, c: (0,) * a.ndim)
    per_b = lambda a: pl.BlockSpec((1,) + a.shape[1:], lambda i, c: (i,) + (0,) * (a.ndim - 1))
    return pl.pallas_call(
        functools.partial(_mlstm_kernel, n_heads=n_heads, valid_len=valid_len),
        grid=(b, s_len // blk),
        in_specs=[zspec(0), zspec(1), zspec(2), zspec(3),
                  pl.BlockSpec((1, blk, 2 * LANES), lambda i, c: (i, c, 0)),
                  per_b(conv0), per_b(c0), per_b(n0), per_b(m0),
                  full(w_conv), full(b_conv), full(gate_bias), full(g_norm)],
        out_specs=[pl.BlockSpec((1, blk, d_b), lambda i, c: (i, c, 0)),
                   per_b(c0), per_b(n0), per_b(m0)],
        out_shape=[jax.ShapeDtypeStruct((b, s_len, d_b), BF16),
                   jax.ShapeDtypeStruct(c0.shape, F32),
                   jax.ShapeDtypeStruct(n0.shape, F32),
                   jax.ShapeDtypeStruct(m0.shape, F32)],
        scratch_shapes=[pltpu.VMEM((blk + SUBLANES, 2 * d_b), F32),
                        pltpu.VMEM((blk, d_b), BF16),
                        pltpu.VMEM((blk, d_b), F32),
                        pltpu.VMEM((n_heads, STATE_ROWS, HEAD_DIM_B), F32),
                        pltpu.VMEM((1, LANES), F32)],
        compiler_params=_params("parallel", "arbitrary"),
        name="mlstm",
    )(z3, z3, z3, z3, gates3, conv0, c0, n0, m0, w_conv, b_conv, gate_bias, g_norm)


def _layer_norm(r, g, b):
    mu = jnp.mean(r, axis=-1, keepdims=True)
    rc = r - mu
    var = jnp.mean(rc * rc, axis=-1, keepdims=True)
    return rc * lax.rsqrt(var + 1e-5) * g + b


def _out_proj_kernel(a_ref, m_ref, x_ref, wa_ref, wm_ref, g_ref, b_ref, x1_ref, x1b_ref, *, alpha):
    mix = _mm(a_ref[...], wa_ref[...]) + _mm(m_ref[...], wm_ref[...])
    y = _layer_norm(alpha * x_ref[...] + mix, g_ref[...], b_ref[...])
    x1_ref[...] = y
    x1b_ref[...] = y.astype(BF16)


def _out_proj(attn2d, mlstm2d, x2d, w_out, ln_g, ln_b, alpha, bm):
    t, d = x2d.shape
    da, db = attn2d.shape[1], mlstm2d.shape[1]
    assert da == db
    row = lambda w: pl.BlockSpec((bm, w), lambda i: (i, 0))
    vec = pl.BlockSpec((1, d), lambda i: (0, 0))
    return pl.pallas_call(
        functools.partial(_out_proj_kernel, alpha=alpha),
        grid=(t // bm,),
        in_specs=[row(da), row(db), row(d),
                  pl.BlockSpec((da, d), lambda i: (0, 0)),
                  pl.BlockSpec((db, d), lambda i: (1, 0)),
                  vec, vec],
        out_specs=[row(d), row(d)],
        out_shape=[jax.ShapeDtypeStruct((t, d), F32), jax.ShapeDtypeStruct((t, d), BF16)],
        compiler_params=_params("parallel"),
        name="out_proj_ln",
    )(attn2d, mlstm2d, x2d, w_out, w_out, ln_g, ln_b)


def _ffn_up_kernel(x_ref, wg_ref, wu_ref, h_ref):
    x = x_ref[...]
    gate = _mm(x, wg_ref[...])
    h_ref[...] = (gate * jax.nn.sigmoid(gate) * _mm(x, wu_ref[...])).astype(h_ref.dtype)


def _ffn_up(x1b, w_gate, w_up, bm, bf):
    t, d = x1b.shape
    f = w_gate.shape[1]
    wspec = pl.BlockSpec((d, bf), lambda i, j: (0, j))
    return pl.pallas_call(
        _ffn_up_kernel,
        grid=(t // bm, f // bf),
        in_specs=[pl.BlockSpec((bm, d), lambda i, j: (i, 0)), wspec, wspec],
        out_specs=pl.BlockSpec((bm, bf), lambda i, j: (i, j)),
        out_shape=jax.ShapeDtypeStruct((t, f), BF16),
        compiler_params=_params("parallel", "arbitrary"),
        name="ffn_up",
    )(x1b, w_gate, w_up)


def _ffn_down_kernel(h_ref, wd_ref, x1_ref, g_ref, b_ref, y_ref, *, alpha):
    y_ref[...] = _layer_norm(alpha * x1_ref[...] + _mm(h_ref[...], wd_ref[...]), g_ref[...], b_ref[...])


def _ffn_down(h, w_down, x1, ln_g, ln_b, alpha, bm):
    t, f = h.shape
    d = w_down.shape[1]
    vec = pl.BlockSpec((1, d), lambda i: (0, 0))
    return pl.pallas_call(
        functools.partial(_ffn_down_kernel, alpha=alpha),
        grid=(t // bm,),
        in_specs=[pl.BlockSpec((bm, f), lambda i: (i, 0)),
                  pl.BlockSpec((f, d), lambda i: (0, 0), pipeline_mode=pl.Buffered(1)),
                  pl.BlockSpec((bm, d), lambda i: (i, 0)),
                  vec, vec],
        out_specs=pl.BlockSpec((bm, d), lambda i: (i, 0)),
        out_shape=jax.ShapeDtypeStruct((t, d), F32),
        compiler_params=_params("parallel"),
        name="ffn_down_ln",
    )(h, w_down, x1, ln_g, ln_b)


def _largest_divisor(n, cap):
    return max(d for d in range(1, min(n, cap) + 1) if n % d == 0)


def _trunk_layer(x, hist_k, hist_v, conv_state, mlstm_state, w, alpha):
    b, s_len, d = x.shape
    t = b * s_len
    c0, n0, m0 = mlstm_state
    n_heads_b = c0.shape[1]
    d_b = n_heads_b * HEAD_DIM_B
    d_attn = w["g_attn"].shape[1]
    n_main = 3 * d_attn + 4 * d_b
    assert d_attn == d_b and w["w_in"].shape[1] == n_main + 2 * n_heads_b

    bm = _largest_divisor(t, 1024)
    x2d = x.reshape(t, d)
    z, gates = _in_proj(x2d, w["w_in"], n_main, w["w_gate"], bm, 1024)
    z3 = z.reshape(b, s_len, -1)

    attn = _attention(z3, hist_k, hist_v, w["bias_pairs"], w["g_attn"])

    conv0 = jnp.pad(conv_state.astype(F32), ((0, 0), (SUBLANES - (CONV_W - 1), 0), (0, 0)))
    m0_row = jnp.pad(m0.astype(F32)[:, None, :], ((0, 0), (0, 0), (0, LANES - n_heads_b)))
    gates3 = gates.reshape(b, s_len, 2 * LANES)
    z3m = z3
    s_pad = -(-s_len // MLSTM_BLK) * MLSTM_BLK
    if s_pad != s_len:
        z3m = jnp.pad(z3, ((0, 0), (0, s_pad - s_len), (0, 0)))
        gates3 = jnp.pad(gates3, ((0, 0), (0, s_pad - s_len), (0, 0)))
    mlstm, c_new, n_new, m_new = _mlstm(
        z3m, gates3, conv0, c0.astype(F32), n0.astype(F32), m0_row,
        w["w_conv"], w["b_conv"], w["gate_bias"], w["g_mlstm"], 3,
        MLSTM_BLK if s_pad == s_len else s_len)
    mlstm = mlstm[:, :s_len]

    bm2 = _largest_divisor(t, 512)
    x1, x1b = _out_proj(attn.reshape(t, d_attn), mlstm.reshape(t, d_b), x2d, w["w_out"],
                        w["ln1_g"], w["ln1_b"], alpha, bm2)
    hmid = _ffn_up(x1b, w["w_ffn_gate"], w["w_ffn_up"], bm, 512)
    y = _ffn_down(hmid, w["w_ffn_down"], x1, w["ln2_g"], w["ln2_b"], alpha, bm2)

    n_heads_a = d_attn // HEAD_DIM_A
    k_new = z3[:, :, d_attn:2 * d_attn]
    v_new = z3[:, :, 2 * d_attn:3 * d_attn]
    qk_raw_tail = z3[:, s_len - (CONV_W - 1):, 3 * d_attn:3 * d_attn + 2 * d_b].astype(F32)
    return (y.reshape(b, s_len, d), k_new, v_new, qk_raw_tail, (c_new, n_new, m_new[:, 0, :n_heads_b]),
            n_heads_a)


def kernel(x_prompt, x_sample, cache_attn_k, cache_attn_v, state_conv, state_mlstm_C, state_mlstm_n, state_mlstm_m, w_in, b_igate, b_fgate, w_conv, b_conv, rel_bias, g_attn_norm, g_mlstm_norm, w_out, ln1_g, ln1_b, w_ffn_gate, w_ffn_up, w_ffn_down, ln2_g, ln2_b):
    depth = w_in.shape[0]
    alpha = (2.0 * depth) ** 0.25
    xp, xs = x_prompt, x_sample
    bp, sp, d_model = xp.shape
    n_heads_b = b_igate.shape[1]
    d_b = n_heads_b * HEAD_DIM_B
    d_attn = g_attn_norm.shape[1]
    n_heads_a = d_attn // HEAD_DIM_A
    n_main = 3 * d_attn + 4 * d_b
    new_p, new_s = [], []
    for l in range(depth):
        w_in_b = w_in[l].astype(BF16)
        lane_pad = lambda a: jnp.pad(a, ((0, 0), (0, LANES - n_heads_b)))
        w = dict(
            w_in=w_in_b,
            w_gate=jnp.concatenate([lane_pad(w_in_b[:, n_main:n_main + n_heads_b]),
                                    lane_pad(w_in_b[:, n_main + n_heads_b:])], axis=1),
            gate_bias=jnp.concatenate([lane_pad(b_igate[l][None, :]), lane_pad(b_fgate[l][None, :])],
                                      axis=1).astype(F32),
            w_conv=w_conv[l], b_conv=b_conv[l][None, :],
            bias_pairs=_bias_pairs(rel_bias[l]),
            g_attn=g_attn_norm[l][None, :], g_mlstm=g_mlstm_norm[l][None, :],
            w_out=w_out[l].astype(BF16), ln1_g=ln1_g[l][None, :], ln1_b=ln1_b[l][None, :],
            w_ffn_gate=w_ffn_gate[l].astype(BF16), w_ffn_up=w_ffn_up[l].astype(BF16),
            w_ffn_down=w_ffn_down[l].astype(BF16), ln2_g=ln2_g[l][None, :], ln2_b=ln2_b[l][None, :],
        )
        zero_state = (jnp.zeros((bp, n_heads_b, HEAD_DIM_B, HEAD_DIM_B), F32),
                      jnp.zeros((bp, n_heads_b, HEAD_DIM_B), F32),
                      jnp.zeros((bp, n_heads_b), F32))
        zero_conv = jnp.zeros((bp, CONV_W - 1, 2 * d_b), F32)
        xp, kp, vp, cp, state_p, _ = _trunk_layer(xp, None, None, zero_conv, zero_state, w, alpha)
        n_keep = min(BAND_ROWS, sp)
        to_heads = lambda a: a.astype(F32).reshape(a.shape[0], a.shape[1], n_heads_a, HEAD_DIM_A)
        new_p.append((to_heads(kp[:, sp - n_keep:]), to_heads(vp[:, sp - n_keep:]), cp) + state_p)
        bs, ls, _ = xs.shape
        ck = cache_attn_k[l].reshape(bs, -1, d_attn)
        cv = cache_attn_v[l].reshape(bs, -1, d_attn)
        xs, ks_, vs, cs, state_s, _ = _trunk_layer(
            xs, ck.astype(BF16), cv.astype(BF16), state_conv[l],
            (state_mlstm_C[l], state_mlstm_n[l], state_mlstm_m[l]), w, alpha)
        n_past = ck.shape[1]
        n_keep_s = min(BAND_ROWS, n_past + ls)
        k_all = jnp.concatenate([ck, ks_.astype(F32)], axis=1)[:, n_past + ls - n_keep_s:]
        v_all = jnp.concatenate([cv, vs.astype(F32)], axis=1)[:, n_past + ls - n_keep_s:]
        new_s.append((to_heads(k_all), to_heads(v_all), cs) + state_s)
    k_p, v_p, conv_p, C_p, n_p, m_p = [jnp.stack(t) for t in zip(*new_p)]
    k_s, v_s, conv_s, C_s, n_s, m_s = [jnp.stack(t) for t in zip(*new_s)]
    return (xp, xs, k_p, v_p, conv_p, C_p, n_p, m_p, k_s, v_s, conv_s, C_s, n_s, m_s)
```

```python
import functools

import jax
import jax.numpy as jnp
from jax import lax
from jax.experimental import pallas as pl
from jax.experimental.pallas import tpu as pltpu

CHUNK = 64
BAND_ROWS = 8 * CHUNK
BAND = BAND_ROWS + CHUNK
REL_CLIP = 256
HEAD_DIM_A = 64
HEAD_DIM_B = 128
CONV_W = 4
LANES = 128
SUBLANES = 8
NEG_BIG = -1e30
LOG2_E = 1.4426950408889634
LN_SPLITS = 4
VMEM_LIMIT_BYTES = 56 * 1024 * 1024

F32 = jnp.float32
BF16 = jnp.bfloat16


def _params(*semantics, flags=None):
    return pltpu.CompilerParams(dimension_semantics=semantics, vmem_limit_bytes=VMEM_LIMIT_BYTES,
                                flags=flags)


def _mm(a, b):
    return jnp.dot(a, b, preferred_element_type=F32)


def _mm_nt(a, b, precision=None):
    return lax.dot_general(a, b, (((1,), (1,)), ((), ())), precision=precision,
                           preferred_element_type=F32)


def _in_proj_kernel(x_ref, w_ref, wg_ref, z_ref, g_ref, xb_ref):
    @pl.when(pl.program_id(1) == 0)
    def _():
        xb = x_ref[...].astype(BF16)
        xb_ref[...] = xb
        g_ref[...] = _mm(xb, wg_ref[...])

    z_ref[...] = _mm(xb_ref[...], w_ref[...]).astype(z_ref.dtype)


def _in_proj(x2d, w_in, n, w_gate, bm, bn):
    t, d = x2d.shape
    ng = w_gate.shape[1]
    return pl.pallas_call(
        _in_proj_kernel,
        grid=(t // bm, n // bn),
        in_specs=[pl.BlockSpec((bm, d), lambda i, j: (i, 0)),
                  pl.BlockSpec((d, bn), lambda i, j: (0, j)),
                  pl.BlockSpec((d, ng), lambda i, j: (0, 0))],
        out_specs=[pl.BlockSpec((bm, bn), lambda i, j: (i, j)),
                   pl.BlockSpec((bm, ng), lambda i, j: (i, 0))],
        out_shape=[jax.ShapeDtypeStruct((t, n), BF16),
                   jax.ShapeDtypeStruct((t, ng), F32)],
        scratch_shapes=[pltpu.VMEM((bm, d), BF16)],
        compiler_params=_params("parallel", "arbitrary"),
        name="in_proj",
    )(x2d, w_in, w_gate)


def _attn_kernel(*refs, has_hist, n_pairs):
    if has_hist:
        q_ref, k_ref, v_ref, kh_ref, vh_ref, bias_ref, g_ref, o_ref, kpad, vpad = refs
    else:
        q_ref, k_ref, v_ref, bias_ref, g_ref, o_ref, kpad, vpad = refs
    c = pl.program_id(1)
    s_len = k_ref.shape[1]

    @pl.when(c == 0)
    def _():
        if has_hist:
            kpad[0:BAND_ROWS, :] = kh_ref[0]
            vpad[0:BAND_ROWS, :] = vh_ref[0]
        else:
            kpad[0:BAND_ROWS, :] = jnp.zeros((BAND_ROWS, kpad.shape[1]), BF16)
            vpad[0:BAND_ROWS, :] = jnp.zeros((BAND_ROWS, vpad.shape[1]), BF16)
        kpad[BAND_ROWS:BAND_ROWS + s_len, :] = k_ref[0]
        vpad[BAND_ROWS:BAND_ROWS + s_len, :] = v_ref[0]

    if has_hist:
        _attn_chunk(q_ref, kpad, vpad, bias_ref, g_ref, o_ref, c, n_pairs, masked=False)
    else:
        @pl.when(c < BAND_ROWS // CHUNK)
        def _():
            _attn_chunk(q_ref, kpad, vpad, bias_ref, g_ref, o_ref, c, n_pairs, masked=True)

        @pl.when(c >= BAND_ROWS // CHUNK)
        def _():
            _attn_chunk(q_ref, kpad, vpad, bias_ref, g_ref, o_ref, c, n_pairs, masked=False)


def _attn_chunk(q_ref, kpad, vpad, bias_ref, g_ref, o_ref, c, n_pairs, masked):
    start = pl.multiple_of(c * CHUNK, CHUNK)
    lane = lax.broadcasted_iota(jnp.int32, (CHUNK, LANES), 1)
    row2 = lax.broadcasted_iota(jnp.int32, (2 * CHUNK, LANES), 0)
    lane2 = lax.broadcasted_iota(jnp.int32, (2 * CHUNK, LANES), 1)
    own_head = (row2 >= CHUNK) == (lane2 >= HEAD_DIM_A)
    if masked:
        col = lax.broadcasted_iota(jnp.int32, (2 * CHUNK, BAND), 1)
        valid = col >= BAND_ROWS - c * CHUNK

    def scores(p):
        cols = slice(p * LANES, (p + 1) * LANES)
        qp = q_ref[0, :, cols]
        zero = jnp.zeros_like(qp)
        q2 = jnp.concatenate([jnp.where(lane < HEAD_DIM_A, qp, zero),
                              jnp.where(lane >= HEAD_DIM_A, qp, zero)], axis=0)
        return _mm_nt(q2, kpad[pl.ds(start, BAND), cols])

    pairs = range(n_pairs)
    s = [scores(p) + bias_ref[p] for p in pairs]
    if masked:
        s = [jnp.where(valid, s[p], NEG_BIG) for p in pairs]
    m = [jnp.max(s[p], axis=-1, keepdims=True) for p in pairs]
    e = [jnp.exp2(s[p] - m[p]) for p in pairs]
    denom = [jnp.sum(e[p], axis=-1, keepdims=True) for p in pairs]
    o = [_mm(e[p].astype(BF16), vpad[pl.ds(start, BAND), p * LANES:(p + 1) * LANES]) for p in pairs]
    o = [jnp.where(own_head, o[p] / denom[p], 0.0) for p in pairs]
    ssq = [jnp.sum(o[p] * o[p], axis=-1, keepdims=True) for p in pairs]
    for p in pairs:
        cols = slice(p * LANES, (p + 1) * LANES)
        on = o[p] * lax.rsqrt(ssq[p] * (1.0 / HEAD_DIM_A) + 1e-6)
        o_ref[0, :, cols] = ((on[:CHUNK] + on[CHUNK:]) * g_ref[:, cols]).astype(o_ref.dtype)


def _attention(z3, hist_k, hist_v, bias_pairs, g_attn):
    b, s_len, _ = z3.shape
    d_attn = g_attn.shape[1]
    n_pairs = d_attn // LANES
    has_hist = hist_k is not None
    in_specs = [pl.BlockSpec((1, CHUNK, d_attn), lambda i, c: (i, c, 0)),
                pl.BlockSpec((1, s_len, d_attn), lambda i, c: (i, 0, 1)),
                pl.BlockSpec((1, s_len, d_attn), lambda i, c: (i, 0, 2))]
    args = [z3, z3, z3]
    if has_hist:
        in_specs += [pl.BlockSpec((1, BAND_ROWS, d_attn), lambda i, c: (i, 0, 0))] * 2
        args += [hist_k, hist_v]
    in_specs += [pl.BlockSpec(bias_pairs.shape, lambda i, c: (0, 0, 0)),
                 pl.BlockSpec((1, d_attn), lambda i, c: (0, 0))]
    args += [bias_pairs, g_attn]
    return pl.pallas_call(
        functools.partial(_attn_kernel, has_hist=has_hist, n_pairs=n_pairs),
        grid=(b, s_len // CHUNK),
        in_specs=in_specs,
        out_specs=pl.BlockSpec((1, CHUNK, d_attn), lambda i, c: (i, c, 0)),
        out_shape=jax.ShapeDtypeStruct((b, s_len, d_attn), BF16),
        scratch_shapes=[pltpu.VMEM((BAND_ROWS + s_len, d_attn), BF16)] * 2,
        compiler_params=_params("parallel", "arbitrary"),
        name="attn_hist" if has_hist else "attn",
    )(*args)


def _bias_pairs(rel_bias):
    n_heads = rel_bias.shape[0]
    n_far = BAND - REL_CLIP
    far = jnp.broadcast_to(rel_bias[:, 2 * REL_CLIP:], (n_heads, n_far))
    near = rel_bias[:, REL_CLIP - (CHUNK - 1):2 * REL_CLIP][:, ::-1]
    ext = jnp.concatenate([far, near], axis=1)
    n_ext = BAND + CHUNK - 1
    flat = jnp.tile(jnp.pad(ext, ((0, 0), (0, 1))), (1, CHUNK))
    bias = flat[:, CHUNK - 1:CHUNK - 1 + CHUNK * n_ext].reshape(n_heads, CHUNK, n_ext)[:, :, :BAND]
    return (bias * LOG2_E).reshape(n_heads // 2, 2 * CHUNK, BAND).astype(F32)


MLSTM_BLK = 128
STATE_ROWS = HEAD_DIM_B + 16


def _log_sigmoid(x):
    return jnp.minimum(x, 0.0) - jnp.log1p(jnp.exp(-jnp.abs(x)))


def _mm_tn(a, b):
    return lax.dot_general(a, b, (((0,), (0,)), ((), ())), preferred_element_type=F32)


def _mlstm_kernel(q_ref, k_ref, v_ref, og_ref, gates_ref, conv0_ref, c0_ref, n0_ref, m0_ref,
                  wconv_ref, bconv_ref, gbias_ref, gnorm_ref,
                  o_ref, c_out_ref, n_out_ref, m_out_ref,
                  xs, qb_s, k_s, ct_s, m_s, *, n_heads, valid_len):
    blk = MLSTM_BLK
    hd = HEAD_DIM_B
    c = pl.program_id(1)
    last = pl.num_programs(1) - 1
    d_b = n_heads * hd
    pad = SUBLANES
    heads = range(n_heads)

    @pl.when(c == 0)
    def _():
        xs[0:pad, :] = conv0_ref[0]
        for h in heads:
            ct_s[h, 0:hd, :] = c0_ref[0, h].T
            ct_s[h, hd:, :] = jnp.zeros((STATE_ROWS - hd, hd), F32)
            ct_s[h, hd:hd + 1, :] = n0_ref[0, h:h + 1, :]
        m_s[...] = m0_ref[0]

    xs[pad:pad + blk, 0:d_b] = q_ref[0].astype(F32)
    xs[pad:pad + blk, d_b:2 * d_b] = k_ref[0].astype(F32)

    gates = gates_ref[0] + gbias_ref[...]
    ig = gates[:, :LANES]
    lf = _log_sigmoid(gates[:, LANES:])
    if valid_len < blk:
        live = lax.broadcasted_iota(jnp.int32, (blk, LANES), 0) < valid_len
        ig = jnp.where(live, ig, NEG_BIG)
        lf = jnp.where(live, lf, 0.0)
    row_i = lax.broadcasted_iota(jnp.int32, (blk, blk), 0)
    col_i = lax.broadcasted_iota(jnp.int32, (blk, blk), 1)
    f_cum = jnp.dot(jnp.where(row_i >= col_i, 1.0, 0.0).astype(F32), lf,
                    precision=lax.Precision.HIGHEST, preferred_element_type=F32)
    sel_rows = -(-n_heads // SUBLANES) * SUBLANES
    sel = jnp.where(lax.broadcasted_iota(jnp.int32, (sel_rows, LANES), 0)
                    == lax.broadcasted_iota(jnp.int32, (sel_rows, LANES), 1), 1.0, 0.0).astype(F32)
    f_rows = _mm_nt(sel, f_cum, precision=lax.Precision.HIGHEST)
    imf = ig - f_cum
    m_prev = m_s[...]
    f_last = f_cum[blk - 1:blk, :]
    g = f_last + imf
    a_last = f_last + m_prev
    m_new = jnp.maximum(a_last, jnp.max(g, axis=0, keepdims=True))
    w_old = jnp.exp(a_last - m_new)
    wk = jnp.exp(g - m_new)

    for j in range(2 * n_heads):
        cols = slice(j * hd, (j + 1) * hd)
        acc = bconv_ref[:, cols]
        for tap in range(CONV_W):
            acc = acc + xs[pl.ds(pad - (CONV_W - 1) + tap, blk), cols] * wconv_ref[tap:tap + 1, cols]
        y = acc * jax.nn.sigmoid(acc)
        if j < n_heads:
            qb_s[:, cols] = y.astype(BF16)
        else:
            k_s[:, (j - n_heads) * hd:(j - n_heads + 1) * hd] = y * (hd ** -0.5)

    hcols = [slice(h * hd, (h + 1) * hd) for h in heads]
    st = [_mm_nt(k_s[:, hcols[h]].astype(BF16), qb_s[:, hcols[h]]) for h in heads]
    inter = [_mm_nt(ct_s[h].astype(BF16), qb_s[:, hcols[h]]) for h in heads]

    visible = row_i <= col_i
    f_row = [f_rows[h:h + 1, :] for h in heads]
    a_row = [f_row[h] + m_prev[:, h:h + 1] for h in heads]
    dt = [jnp.where(visible, jnp.broadcast_to(imf[:, h:h + 1], (blk, blk)) + f_row[h], NEG_BIG) for h in heads]
    m_t = [jnp.maximum(a_row[h], jnp.max(dt[h], axis=0, keepdims=True)) for h in heads]
    pt = [st[h] * jnp.exp(dt[h] - m_t[h]) for h in heads]
    w_inter = [jnp.exp(a_row[h] - m_t[h]) for h in heads]
    den = [jnp.sum(pt[h], axis=0, keepdims=True) + w_inter[h] * inter[h][hd:hd + 1, :] for h in heads]
    scale = [1.0 / jnp.maximum(jnp.abs(den[h]), jnp.exp(-m_t[h])) for h in heads]
    kw = [k_s[:, hcols[h]] * jnp.broadcast_to(wk[:, h:h + 1], (blk, hd)) for h in heads]

    num_t = [_mm_tn(v_ref[0, :, hcols[h]], pt[h].astype(BF16)) for h in heads]
    dct = [_mm_tn(v_ref[0, :, hcols[h]], kw[h].astype(BF16)) for h in heads]

    ht = [(num_t[h] + w_inter[h] * inter[h][0:hd, :]) * scale[h] for h in heads]
    ssq = [jnp.sum(ht[h] * ht[h], axis=0, keepdims=True) for h in heads]
    for h in heads:
        hn = (ht[h] * lax.rsqrt(ssq[h] * (1.0 / hd) + 1e-6)).T
        out = hn * gnorm_ref[:, hcols[h]] * jax.nn.sigmoid(og_ref[0, :, hcols[h]].astype(F32))
        o_ref[0, :, hcols[h]] = out.astype(o_ref.dtype)
        decay = w_old[:, h:h + 1]
        ct_s[h, 0:hd, :] = decay * ct_s[h, 0:hd, :] + dct[h]
        ct_s[h, hd:hd + 1, :] = decay * ct_s[h, hd:hd + 1, :] + jnp.sum(kw[h], axis=0, keepdims=True)
    m_s[...] = m_new
    xs[0:pad, :] = xs[blk:blk + pad, :]

    @pl.when(c == last)
    def _():
        for h in heads:
            c_out_ref[0, h] = ct_s[h, 0:hd, :].T
            n_out_ref[0, h:h + 1, :] = ct_s[h, hd:hd + 1, :]
        m_out_ref[0] = m_s[...]


def _mlstm(z3, gates3, conv0, c0, n0, m0, w_conv, b_conv, gate_bias, g_norm, col_block0, valid_len):
    b, s_len, _ = z3.shape
    n_heads = c0.shape[1]
    d_b = n_heads * HEAD_DIM_B
    blk = MLSTM_BLK
    assert s_len % blk == 0 and n_heads <= LANES and (valid_len == blk or s_len == blk)
    zspec = lambda k: pl.BlockSpec((1, blk, d_b), lambda i, c: (i, c, col_block0 + k))
    full = lambda a: pl.BlockSpec(a.shape, lambda i, c: (0,) * a.ndim)
    per_b = lambda a: pl.BlockSpec((1,) + a.shape[1:], lambda i, c: (i,) + (0,) * (a.ndim - 1))
    return pl.pallas_call(
        functools.partial(_mlstm_kernel, n_heads=n_heads, valid_len=valid_len),
        grid=(b, s_len // blk),
        in_specs=[zspec(0), zspec(1), zspec(2), zspec(3),
                  pl.BlockSpec((1, blk, 2 * LANES), lambda i, c: (i, c, 0)),
                  per_b(conv0), per_b(c0), per_b(n0), per_b(m0),
                  full(w_conv), full(b_conv), full(gate_bias), full(g_norm)],
        out_specs=[pl.BlockSpec((1, blk, d_b), lambda i, c: (i, c, 0)),
                   per_b(c0), per_b(n0), per_b(m0)],
        out_shape=[jax.ShapeDtypeStruct((b, s_len, d_b), BF16),
                   jax.ShapeDtypeStruct(c0.shape, F32),
                   jax.ShapeDtypeStruct(n0.shape, F32),
                   jax.ShapeDtypeStruct(m0.shape, F32)],
        scratch_shapes=[pltpu.VMEM((blk + SUBLANES, 2 * d_b), F32),
                        pltpu.VMEM((blk, d_b), BF16),
                        pltpu.VMEM((blk, d_b), F32),
                        pltpu.VMEM((n_heads, STATE_ROWS, HEAD_DIM_B), F32),
                        pltpu.VMEM((1, LANES), F32)],
        compiler_params=_params("parallel", "arbitrary"),
        name="mlstm",
    )(z3, z3, z3, z3, gates3, conv0, c0, n0, m0, w_conv, b_conv, gate_bias, g_norm)


def _layer_norm(r, g, b):
    mu = jnp.mean(r, axis=-1, keepdims=True)
    rc = r - mu
    var = jnp.mean(rc * rc, axis=-1, keepdims=True)
    return rc * lax.rsqrt(var + 1e-5) * g + b


def _row_splits(n_rows):
    sub = n_rows // LN_SPLITS if n_rows % (LN_SPLITS * 2 * SUBLANES) == 0 else n_rows
    return [slice(r, r + sub) for r in range(0, n_rows, sub)]


def _out_proj_kernel(a_ref, m_ref, x_ref, wa_ref, wm_ref, g_ref, b_ref, x1_ref, x1b_ref, *, alpha):
    for rows in _row_splits(x_ref.shape[0]):
        mix = _mm(a_ref[rows, :], wa_ref[...]) + _mm(m_ref[rows, :], wm_ref[...])
        y = _layer_norm(alpha * x_ref[rows, :] + mix, g_ref[...], b_ref[...])
        x1_ref[rows, :] = y
        x1b_ref[rows, :] = y.astype(BF16)


def _out_proj(attn2d, mlstm2d, x2d, w_out, ln_g, ln_b, alpha, bm):
    t, d = x2d.shape
    da, db = attn2d.shape[1], mlstm2d.shape[1]
    assert da == db
    row = lambda w: pl.BlockSpec((bm, w), lambda i: (i, 0))
    vec = pl.BlockSpec((1, d), lambda i: (0, 0))
    return pl.pallas_call(
        functools.partial(_out_proj_kernel, alpha=alpha),
        grid=(t // bm,),
        in_specs=[row(da), row(db), row(d),
                  pl.BlockSpec((da, d), lambda i: (0, 0)),
                  pl.BlockSpec((db, d), lambda i: (1, 0)),
                  vec, vec],
        out_specs=[row(d), row(d)],
        out_shape=[jax.ShapeDtypeStruct((t, d), F32), jax.ShapeDtypeStruct((t, d), BF16)],
        compiler_params=_params("parallel"),
        name="out_proj_ln",
    )(attn2d, mlstm2d, x2d, w_out, w_out, ln_g, ln_b)


def _ffn_up_kernel(x_ref, wg_ref, wu_ref, h_ref):
    x = x_ref[...]
    gate = _mm(x, wg_ref[...])
    h_ref[...] = (gate * jax.nn.sigmoid(gate) * _mm(x, wu_ref[...])).astype(h_ref.dtype)


def _ffn_up(x1b, w_gate, w_up, bm, bf):
    t, d = x1b.shape
    f = w_gate.shape[1]
    wspec = pl.BlockSpec((d, bf), lambda i, j: (0, j))
    return pl.pallas_call(
        _ffn_up_kernel,
        grid=(t // bm, f // bf),
        in_specs=[pl.BlockSpec((bm, d), lambda i, j: (i, 0)), wspec, wspec],
        out_specs=pl.BlockSpec((bm, bf), lambda i, j: (i, j)),
        out_shape=jax.ShapeDtypeStruct((t, f), BF16),
        compiler_params=_params("parallel", "arbitrary"),
        name="ffn_up",
    )(x1b, w_gate, w_up)


def _ffn_down_kernel(h_ref, wd_ref, x1_ref, g_ref, b_ref, y_ref, *, alpha):
    for rows in _row_splits(x1_ref.shape[0]):
        y_ref[rows, :] = _layer_norm(alpha * x1_ref[rows, :] + _mm(h_ref[rows, :], wd_ref[...]),
                                     g_ref[...], b_ref[...])


def _ffn_down(h, w_down, x1, ln_g, ln_b, alpha, bm):
    t, f = h.shape
    d = w_down.shape[1]
    vec = pl.BlockSpec((1, d), lambda i: (0, 0))
    return pl.pallas_call(
        functools.partial(_ffn_down_kernel, alpha=alpha),
        grid=(t // bm,),
        in_specs=[pl.BlockSpec((bm, f), lambda i: (i, 0)),
                  pl.BlockSpec((f, d), lambda i: (0, 0), pipeline_mode=pl.Buffered(1)),
                  pl.BlockSpec((bm, d), lambda i: (i, 0)),
                  vec, vec],
        out_specs=pl.BlockSpec((bm, d), lambda i: (i, 0)),
        out_shape=jax.ShapeDtypeStruct((t, d), F32),
        compiler_params=_params("parallel"),
        name="ffn_down_ln",
    )(h, w_down, x1, ln_g, ln_b)


def _largest_divisor(n, cap):
    return max(d for d in range(1, min(n, cap) + 1) if n % d == 0)


def _trunk_layer(x, hist_k, hist_v, conv_state, mlstm_state, w, alpha):
    b, s_len, d = x.shape
    t = b * s_len
    c0, n0, m0 = mlstm_state
    n_heads_b = c0.shape[1]
    d_b = n_heads_b * HEAD_DIM_B
    d_attn = w["g_attn"].shape[1]
    n_main = 3 * d_attn + 4 * d_b
    assert d_attn == d_b and w["w_in"].shape[1] == n_main

    bm = _largest_divisor(t, 1024)
    x2d = x.reshape(t, d)
    z, gates = _in_proj(x2d, w["w_in"], n_main, w["w_gate"], bm, _largest_divisor(n_main // LANES, 14) * LANES)
    z3 = z.reshape(b, s_len, -1)

    attn = _attention(z3, hist_k, hist_v, w["bias_pairs"], w["g_attn"])

    conv0 = jnp.pad(conv_state.astype(F32), ((0, 0), (SUBLANES - (CONV_W - 1), 0), (0, 0)))
    m0_row = jnp.pad(m0.astype(F32)[:, None, :], ((0, 0), (0, 0), (0, LANES - n_heads_b)))
    gates3 = gates.reshape(b, s_len, 2 * LANES)
    z3m = z3
    s_pad = -(-s_len // MLSTM_BLK) * MLSTM_BLK
    if s_pad != s_len:
        z3m = jnp.pad(z3, ((0, 0), (0, s_pad - s_len), (0, 0)))
        gates3 = jnp.pad(gates3, ((0, 0), (0, s_pad - s_len), (0, 0)))
    mlstm, c_new, n_new, m_new = _mlstm(
        z3m, gates3, conv0, c0.astype(F32), n0.astype(F32), m0_row,
        w["w_conv"], w["b_conv"], w["gate_bias"], w["g_mlstm"], 3,
        MLSTM_BLK if s_pad == s_len else s_len)
    mlstm = mlstm[:, :s_len]

    bm2 = _largest_divisor(t, 512)
    x1, x1b = _out_proj(attn.reshape(t, d_attn), mlstm.reshape(t, d_b), x2d, w["w_out"],
                        w["ln1_g"], w["ln1_b"], alpha, bm2)
    d_ff = w["w_ffn_gate"].shape[1]
    hmid = _ffn_up(x1b, w["w_ffn_gate"], w["w_ffn_up"], bm, _largest_divisor(d_ff // LANES, 11) * LANES)
    y = _ffn_down(hmid, w["w_ffn_down"], x1, w["ln2_g"], w["ln2_b"], alpha, bm2)

    n_heads_a = d_attn // HEAD_DIM_A
    k_new = z3[:, :, d_attn:2 * d_attn]
    v_new = z3[:, :, 2 * d_attn:3 * d_attn]
    qk_raw_tail = z3[:, s_len - (CONV_W - 1):, 3 * d_attn:3 * d_attn + 2 * d_b].astype(F32)
    return (y.reshape(b, s_len, d), k_new, v_new, qk_raw_tail, (c_new, n_new, m_new[:, 0, :n_heads_b]),
            n_heads_a)


def kernel(x_prompt, x_sample, cache_attn_k, cache_attn_v, state_conv, state_mlstm_C, state_mlstm_n, state_mlstm_m, w_in, b_igate, b_fgate, w_conv, b_conv, rel_bias, g_attn_norm, g_mlstm_norm, w_out, ln1_g, ln1_b, w_ffn_gate, w_ffn_up, w_ffn_down, ln2_g, ln2_b):
    depth = w_in.shape[0]
    alpha = (2.0 * depth) ** 0.25
    xp, xs = x_prompt, x_sample
    bp, sp, d_model = xp.shape
    n_heads_b = b_igate.shape[1]
    d_b = n_heads_b * HEAD_DIM_B
    d_attn = g_attn_norm.shape[1]
    n_heads_a = d_attn // HEAD_DIM_A
    n_main = 3 * d_attn + 4 * d_b
    new_p, new_s = [], []
    for l in range(depth):
        q_scale = jnp.where(jnp.arange(n_main) < d_attn, (HEAD_DIM_A ** -0.5) * LOG2_E, 1.0).astype(F32)
        w_in_b = (w_in[l][:, :n_main] * q_scale[None, :]).astype(BF16)
        w_gates = w_in[l][:, n_main:].astype(BF16)
        lane_pad = lambda a: jnp.pad(a, ((0, 0), (0, LANES - n_heads_b)))
        w = dict(
            w_in=w_in_b,
            w_gate=jnp.concatenate([lane_pad(w_gates[:, :n_heads_b]), lane_pad(w_gates[:, n_heads_b:])],
                                   axis=1),
            gate_bias=jnp.concatenate([lane_pad(b_igate[l][None, :]), lane_pad(b_fgate[l][None, :])],
                                      axis=1).astype(F32),
            w_conv=w_conv[l], b_conv=b_conv[l][None, :],
            bias_pairs=_bias_pairs(rel_bias[l]),
            g_attn=g_attn_norm[l][None, :], g_mlstm=g_mlstm_norm[l][None, :],
            w_out=w_out[l].astype(BF16), ln1_g=ln1_g[l][None, :], ln1_b=ln1_b[l][None, :],
            w_ffn_gate=w_ffn_gate[l].astype(BF16), w_ffn_up=w_ffn_up[l].astype(BF16),
            w_ffn_down=w_ffn_down[l].astype(BF16), ln2_g=ln2_g[l][None, :], ln2_b=ln2_b[l][None, :],
        )
        zero_state = (jnp.zeros((bp, n_heads_b, HEAD_DIM_B, HEAD_DIM_B), F32),
                      jnp.zeros((bp, n_heads_b, HEAD_DIM_B), F32),
                      jnp.zeros((bp, n_heads_b), F32))
        zero_conv = jnp.zeros((bp, CONV_W - 1, 2 * d_b), F32)
        xp, kp, vp, cp, state_p, _ = _trunk_layer(xp, None, None, zero_conv, zero_state, w, alpha)
        n_keep = min(BAND_ROWS, sp)
        to_heads = lambda a: a.astype(F32).reshape(a.shape[0], a.shape[1], n_heads_a, HEAD_DIM_A)
        new_p.append((to_heads(kp[:, sp - n_keep:]), to_heads(vp[:, sp - n_keep:]), cp) + state_p)
        bs, ls, _ = xs.shape
        ck = cache_attn_k[l].reshape(bs, -1, d_attn)
        cv = cache_attn_v[l].reshape(bs, -1, d_attn)
        xs, ks_, vs, cs, state_s, _ = _trunk_layer(
            xs, ck.astype(BF16), cv.astype(BF16), state_conv[l],
            (state_mlstm_C[l], state_mlstm_n[l], state_mlstm_m[l]), w, alpha)
        n_past = ck.shape[1]
        n_keep_s = min(BAND_ROWS, n_past + ls)
        k_all = jnp.concatenate([ck, ks_.astype(F32)], axis=1)[:, n_past + ls - n_keep_s:]
        v_all = jnp.concatenate([cv, vs.astype(F32)], axis=1)[:, n_past + ls - n_keep_s:]
        new_s.append((to_heads(k_all), to_heads(v_all), cs) + state_s)
    k_p, v_p, conv_p, C_p, n_p, m_p = [jnp.stack(t) for t in zip(*new_p)]
    k_s, v_s, conv_s, C_s, n_s, m_s = [jnp.stack(t) for t in zip(*new_s)]
    return (xp, xs, k_p, v_p, conv_p, C_p, n_p, m_p, k_s, v_s, conv_s, C_s, n_s, m_s)
```

```python
import functools

import jax
import jax.numpy as jnp
from jax import lax
from jax.experimental import pallas as pl
from jax.experimental.pallas import tpu as pltpu

CHUNK = 64
BAND_ROWS = 8 * CHUNK
BAND = BAND_ROWS + CHUNK
REL_CLIP = 256
HEAD_DIM_A = 64
HEAD_DIM_B = 128
CONV_W = 4
LANES = 128
SUBLANES = 8
NEG_BIG = -1e30
LOG2_E = 1.4426950408889634
LN_SPLITS = 4
ATTN_CHUNKS_PER_STEP = 2
VMEM_LIMIT_BYTES = 56 * 1024 * 1024

F32 = jnp.float32
BF16 = jnp.bfloat16


def _params(*semantics, flags=None):
    return pltpu.CompilerParams(dimension_semantics=semantics, vmem_limit_bytes=VMEM_LIMIT_BYTES,
                                flags=flags)


def _mm(a, b):
    return jnp.dot(a, b, preferred_element_type=F32)


def _mm_nt(a, b, precision=None):
    return lax.dot_general(a, b, (((1,), (1,)), ((), ())), precision=precision,
                           preferred_element_type=F32)


def _in_proj_kernel(x_ref, w_ref, wg_ref, z_ref, g_ref, xb_ref, *, n_heads):
    @pl.when(pl.program_id(1) == 0)
    def _():
        xb = x_ref[...].astype(BF16)
        xb_ref[...] = xb
        lane_w = lax.broadcasted_iota(jnp.int32, wg_ref.shape, 1)
        wg = jnp.where(lane_w < 2 * n_heads, wg_ref[...], 0.0).astype(BF16)
        g = _mm(xb, wg)
        lane = lax.broadcasted_iota(jnp.int32, g.shape, 1)
        forget = jnp.where(lane < n_heads, pltpu.roll(g, LANES - n_heads, axis=1), 0.0)
        g_ref[...] = jnp.concatenate([jnp.where(lane < n_heads, g, 0.0), forget], axis=1)

    z_ref[...] = _mm(xb_ref[...], w_ref[...]).astype(z_ref.dtype)


def _in_proj(x2d, w_main, w_full, n_heads, bm, bn):
    t, d = x2d.shape
    n = w_main.shape[1]
    assert n % LANES == 0 and w_full.shape[1] == n + 2 * n_heads and 2 * n_heads <= LANES
    return pl.pallas_call(
        functools.partial(_in_proj_kernel, n_heads=n_heads),
        grid=(t // bm, n // bn),
        in_specs=[pl.BlockSpec((bm, d), lambda i, j: (i, 0)),
                  pl.BlockSpec((d, bn), lambda i, j: (0, j)),
                  pl.BlockSpec((d, LANES), lambda i, j: (0, n // LANES))],
        out_specs=[pl.BlockSpec((bm, bn), lambda i, j: (i, j)),
                   pl.BlockSpec((bm, 2 * LANES), lambda i, j: (i, 0))],
        out_shape=[jax.ShapeDtypeStruct((t, n), BF16),
                   jax.ShapeDtypeStruct((t, 2 * LANES), F32)],
        scratch_shapes=[pltpu.VMEM((bm, d), BF16)],
        compiler_params=_params("parallel", "arbitrary"),
        name="in_proj",
    )(x2d, w_main, w_full)


def _attn_kernel(*refs, has_hist, n_pairs, n_sub):
    if has_hist:
        q_ref, k_ref, v_ref, kh_ref, vh_ref, bias_ref, g_ref, o_ref, kt_ref, vt_ref, kpad, vpad = refs
    else:
        q_ref, k_ref, v_ref, bias_ref, g_ref, o_ref, kt_ref, vt_ref, kpad, vpad = refs
    step = pl.program_id(1)
    s_len = k_ref.shape[1]
    n_keep = kt_ref.shape[1]
    n_new = min(n_keep, s_len)

    @pl.when(step == 0)
    def _():
        if has_hist:
            kpad[0:BAND_ROWS, :] = kh_ref[0].astype(BF16)
            vpad[0:BAND_ROWS, :] = vh_ref[0].astype(BF16)
            if n_new < n_keep:
                kt_ref[0, 0:n_keep - n_new, :] = kh_ref[0, BAND_ROWS - (n_keep - n_new):, :]
                vt_ref[0, 0:n_keep - n_new, :] = vh_ref[0, BAND_ROWS - (n_keep - n_new):, :]
        else:
            kpad[0:BAND_ROWS, :] = jnp.zeros((BAND_ROWS, kpad.shape[1]), BF16)
            vpad[0:BAND_ROWS, :] = jnp.zeros((BAND_ROWS, vpad.shape[1]), BF16)
        kpad[BAND_ROWS:BAND_ROWS + s_len, :] = k_ref[0]
        vpad[BAND_ROWS:BAND_ROWS + s_len, :] = v_ref[0]
        kt_ref[0, n_keep - n_new:, :] = k_ref[0, s_len - n_new:, :].astype(F32)
        vt_ref[0, n_keep - n_new:, :] = v_ref[0, s_len - n_new:, :].astype(F32)

    def chunks(masked):
        for u in range(n_sub):
            _attn_chunk(q_ref, kpad, vpad, bias_ref, g_ref, o_ref, step * n_sub + u, u, n_pairs, masked)

    if has_hist:
        chunks(masked=False)
    else:
        first_steps = BAND_ROWS // CHUNK // n_sub
        pl.when(step < first_steps)(functools.partial(chunks, masked=True))
        pl.when(step >= first_steps)(functools.partial(chunks, masked=False))


def _attn_chunk(q_ref, kpad, vpad, bias_ref, g_ref, o_ref, c, u, n_pairs, masked):
    start = pl.multiple_of(c * CHUNK, CHUNK)
    rows = slice(u * CHUNK, (u + 1) * CHUNK)
    lane = lax.broadcasted_iota(jnp.int32, (CHUNK, LANES), 1)
    row2 = lax.broadcasted_iota(jnp.int32, (2 * CHUNK, LANES), 0)
    lane2 = lax.broadcasted_iota(jnp.int32, (2 * CHUNK, LANES), 1)
    own_head = (row2 >= CHUNK) == (lane2 >= HEAD_DIM_A)
    if masked:
        col = lax.broadcasted_iota(jnp.int32, (2 * CHUNK, BAND), 1)
        valid = col >= BAND_ROWS - c * CHUNK

    def scores(p):
        cols = slice(p * LANES, (p + 1) * LANES)
        qp = q_ref[0, rows, cols]
        zero = jnp.zeros_like(qp)
        q2 = jnp.concatenate([jnp.where(lane < HEAD_DIM_A, qp, zero),
                              jnp.where(lane >= HEAD_DIM_A, qp, zero)], axis=0)
        return _mm_nt(q2, kpad[pl.ds(start, BAND), cols])

    pairs = range(n_pairs)
    s = [scores(p) + bias_ref[p] for p in pairs]
    if masked:
        s = [jnp.where(valid, s[p], NEG_BIG) for p in pairs]
    m = [jnp.max(s[p], axis=-1, keepdims=True) for p in pairs]
    e = [jnp.exp2((s[p] - m[p]).astype(BF16)) for p in pairs]
    ones = jnp.ones((BAND, LANES), BF16)
    o = [_mm(e[p], jnp.concatenate([vpad[pl.ds(start, BAND), p * LANES:(p + 1) * LANES], ones], axis=1))
         for p in pairs]
    o = [jnp.where(own_head, o[p][:, :LANES] / o[p][:, LANES:], 0.0) for p in pairs]
    ssq = [jnp.sum(o[p] * o[p], axis=-1, keepdims=True) for p in pairs]
    for p in pairs:
        cols = slice(p * LANES, (p + 1) * LANES)
        on = o[p] * lax.rsqrt(ssq[p] * (1.0 / HEAD_DIM_A) + 1e-6)
        o_ref[0, rows, cols] = ((on[:CHUNK] + on[CHUNK:]) * g_ref[:, cols]).astype(o_ref.dtype)


def _attention(z3, hist_k, hist_v, bias_pairs, g_attn):
    b, s_len, _ = z3.shape
    d_attn = g_attn.shape[1]
    n_pairs = d_attn // LANES
    has_hist = hist_k is not None
    n_keep = BAND_ROWS if has_hist else min(BAND_ROWS, s_len)
    n_chunks = s_len // CHUNK
    n_sub = ATTN_CHUNKS_PER_STEP if n_chunks % ATTN_CHUNKS_PER_STEP == 0 else 1
    assert (BAND_ROWS // CHUNK) % n_sub == 0
    in_specs = [pl.BlockSpec((1, n_sub * CHUNK, d_attn), lambda i, c: (i, c, 0)),
                pl.BlockSpec((1, s_len, d_attn), lambda i, c: (i, 0, 1)),
                pl.BlockSpec((1, s_len, d_attn), lambda i, c: (i, 0, 2))]
    args = [z3, z3, z3]
    if has_hist:
        in_specs += [pl.BlockSpec((1, BAND_ROWS, d_attn), lambda i, c: (i, 0, 0))] * 2
        args += [hist_k, hist_v]
    in_specs += [pl.BlockSpec(bias_pairs.shape, lambda i, c: (0, 0, 0)),
                 pl.BlockSpec((1, d_attn), lambda i, c: (0, 0))]
    args += [bias_pairs, g_attn]
    tail_spec = pl.BlockSpec((1, n_keep, d_attn), lambda i, c: (i, 0, 0))
    tail_shape = jax.ShapeDtypeStruct((b, n_keep, d_attn), F32)
    return pl.pallas_call(
        functools.partial(_attn_kernel, has_hist=has_hist, n_pairs=n_pairs, n_sub=n_sub),
        grid=(b, n_chunks // n_sub),
        in_specs=in_specs,
        out_specs=[pl.BlockSpec((1, n_sub * CHUNK, d_attn), lambda i, c: (i, c, 0)), tail_spec, tail_spec],
        out_shape=[jax.ShapeDtypeStruct((b, s_len, d_attn), BF16), tail_shape, tail_shape],
        scratch_shapes=[pltpu.VMEM((BAND_ROWS + s_len, d_attn), BF16)] * 2,
        compiler_params=_params("parallel", "arbitrary"),
        name="attn_hist" if has_hist else "attn",
    )(*args)


def _bias_pairs(rel_bias):
    n_heads = rel_bias.shape[0]
    n_far = BAND - REL_CLIP
    far = jnp.broadcast_to(rel_bias[:, 2 * REL_CLIP:], (n_heads, n_far))
    near = rel_bias[:, REL_CLIP - (CHUNK - 1):2 * REL_CLIP][:, ::-1]
    ext = jnp.concatenate([far, near], axis=1)
    n_ext = BAND + CHUNK - 1
    flat = jnp.tile(jnp.pad(ext, ((0, 0), (0, 1))), (1, CHUNK))
    bias = flat[:, CHUNK - 1:CHUNK - 1 + CHUNK * n_ext].reshape(n_heads, CHUNK, n_ext)[:, :, :BAND]
    return (bias * LOG2_E).reshape(n_heads // 2, 2 * CHUNK, BAND).astype(F32)


MLSTM_BLK = 128
STATE_ROWS = HEAD_DIM_B + 16


def _log_sigmoid(x):
    return jnp.minimum(x, 0.0) - jnp.log1p(jnp.exp(-jnp.abs(x)))


def _mm_tn(a, b):
    return lax.dot_general(a, b, (((0,), (0,)), ((), ())), preferred_element_type=F32)


def _mlstm_kernel(q_ref, k_ref, v_ref, og_ref, gates_ref, conv0_ref, c0_ref, n0_ref, m0_ref,
                  wconv_ref, bconv_ref, gbias_ref, gnorm_ref,
                  o_ref, c_out_ref, n_out_ref, m_out_ref,
                  xs, qb_s, k_s, ct_s, m_s, *, n_heads, valid_len):
    blk = MLSTM_BLK
    hd = HEAD_DIM_B
    c = pl.program_id(1)
    last = pl.num_programs(1) - 1
    d_b = n_heads * hd
    pad = SUBLANES
    heads = range(n_heads)

    @pl.when(c == 0)
    def _():
        xs[0:pad, :] = conv0_ref[0]
        for h in heads:
            ct_s[h, 0:hd, :] = c0_ref[0, h].T
            ct_s[h, hd:, :] = jnp.zeros((STATE_ROWS - hd, hd), F32)
            ct_s[h, hd:hd + 1, :] = n0_ref[0, h:h + 1, :]
        m_s[...] = m0_ref[0]

    xs[pad:pad + blk, 0:d_b] = q_ref[0].astype(F32)
    xs[pad:pad + blk, d_b:2 * d_b] = k_ref[0].astype(F32)

    gates = gates_ref[0] + gbias_ref[...]
    ig = gates[:, :LANES]
    lf = _log_sigmoid(gates[:, LANES:])
    if valid_len < blk:
        live = lax.broadcasted_iota(jnp.int32, (blk, LANES), 0) < valid_len
        ig = jnp.where(live, ig, NEG_BIG)
        lf = jnp.where(live, lf, 0.0)
    row_i = lax.broadcasted_iota(jnp.int32, (blk, blk), 0)
    col_i = lax.broadcasted_iota(jnp.int32, (blk, blk), 1)
    f_cum = jnp.dot(jnp.where(row_i >= col_i, 1.0, 0.0).astype(F32), lf,
                    precision=lax.Precision.HIGHEST, preferred_element_type=F32)
    sel_rows = -(-n_heads // SUBLANES) * SUBLANES
    sel = jnp.where(lax.broadcasted_iota(jnp.int32, (sel_rows, LANES), 0)
                    == lax.broadcasted_iota(jnp.int32, (sel_rows, LANES), 1), 1.0, 0.0).astype(F32)
    f_rows = _mm_nt(sel, f_cum, precision=lax.Precision.HIGHEST)
    imf = ig - f_cum
    m_prev = m_s[...]
    f_last = f_cum[blk - 1:blk, :]
    g = f_last + imf
    a_last = f_last + m_prev
    m_new = jnp.maximum(a_last, jnp.max(g, axis=0, keepdims=True))
    w_old = jnp.exp(a_last - m_new)
    wk = jnp.exp(g - m_new)

    for j in range(2 * n_heads):
        cols = slice(j * hd, (j + 1) * hd)
        acc = bconv_ref[:, cols]
        for tap in range(CONV_W):
            acc = acc + xs[pl.ds(pad - (CONV_W - 1) + tap, blk), cols] * wconv_ref[tap:tap + 1, cols]
        y = acc * jax.nn.sigmoid(acc)
        if j < n_heads:
            qb_s[:, cols] = y.astype(BF16)
        else:
            k_s[:, (j - n_heads) * hd:(j - n_heads + 1) * hd] = y * (hd ** -0.5)

    hcols = [slice(h * hd, (h + 1) * hd) for h in heads]
    st = [_mm_nt(k_s[:, hcols[h]].astype(BF16), qb_s[:, hcols[h]]) for h in heads]
    inter = [_mm_nt(ct_s[h].astype(BF16), qb_s[:, hcols[h]]) for h in heads]

    visible = row_i <= col_i
    f_row = [f_rows[h:h + 1, :] for h in heads]
    a_row = [f_row[h] + m_prev[:, h:h + 1] for h in heads]
    dt = [jnp.where(visible, jnp.broadcast_to(imf[:, h:h + 1], (blk, blk)) + f_row[h], NEG_BIG) for h in heads]
    m_t = [jnp.maximum(a_row[h], jnp.max(dt[h], axis=0, keepdims=True)) for h in heads]
    pt = [st[h] * jnp.exp(dt[h] - m_t[h]) for h in heads]
    w_inter = [jnp.exp(a_row[h] - m_t[h]) for h in heads]
    den = [jnp.sum(pt[h], axis=0, keepdims=True) + w_inter[h] * inter[h][hd:hd + 1, :] for h in heads]
    scale = [1.0 / jnp.maximum(jnp.abs(den[h]), jnp.exp(-m_t[h])) for h in heads]
    kw = [k_s[:, hcols[h]] * jnp.broadcast_to(wk[:, h:h + 1], (blk, hd)) for h in heads]

    num_t = [_mm_tn(v_ref[0, :, hcols[h]], pt[h].astype(BF16)) for h in heads]
    dct = [_mm_tn(v_ref[0, :, hcols[h]], kw[h].astype(BF16)) for h in heads]

    ht = [(num_t[h] + w_inter[h] * inter[h][0:hd, :]) * scale[h] for h in heads]
    ssq = [jnp.sum(ht[h] * ht[h], axis=0, keepdims=True) for h in heads]
    for h in heads:
        hn = (ht[h] * lax.rsqrt(ssq[h] * (1.0 / hd) + 1e-6)).T
        out = hn * gnorm_ref[:, hcols[h]] * jax.nn.sigmoid(og_ref[0, :, hcols[h]].astype(F32))
        o_ref[0, :, hcols[h]] = out.astype(o_ref.dtype)
        decay = w_old[:, h:h + 1]
        ct_s[h, 0:hd, :] = decay * ct_s[h, 0:hd, :] + dct[h]
        ct_s[h, hd:hd + 1, :] = decay * ct_s[h, hd:hd + 1, :] + jnp.sum(kw[h], axis=0, keepdims=True)
    m_s[...] = m_new
    xs[0:pad, :] = xs[blk:blk + pad, :]

    @pl.when(c == last)
    def _():
        for h in heads:
            c_out_ref[0, h] = ct_s[h, 0:hd, :].T
            n_out_ref[0, h:h + 1, :] = ct_s[h, hd:hd + 1, :]
        m_out_ref[0] = m_s[...]


def _mlstm(z3, gates3, conv0, c0, n0, m0, w_conv, b_conv, gate_bias, g_norm, col_block0, valid_len):
    b, s_len, _ = z3.shape
    n_heads = c0.shape[1]
    d_b = n_heads * HEAD_DIM_B
    blk = MLSTM_BLK
    assert s_len % blk == 0 and n_heads <= LANES and (valid_len == blk or s_len == blk)
    zspec = lambda k: pl.BlockSpec((1, blk, d_b), lambda i, c: (i, c, col_block0 + k))
    full = lambda a: pl.BlockSpec(a.shape, lambda i, c: (0,) * a.ndim)
    per_b = lambda a: pl.BlockSpec((1,) + a.shape[1:], lambda i, c: (i,) + (0,) * (a.ndim - 1))
    return pl.pallas_call(
        functools.partial(_mlstm_kernel, n_heads=n_heads, valid_len=valid_len),
        grid=(b, s_len // blk),
        in_specs=[zspec(0), zspec(1), zspec(2), zspec(3),
                  pl.BlockSpec((1, blk, 2 * LANES), lambda i, c: (i, c, 0)),
                  per_b(conv0), per_b(c0), per_b(n0), per_b(m0),
                  full(w_conv), full(b_conv), full(gate_bias), full(g_norm)],
        out_specs=[pl.BlockSpec((1, blk, d_b), lambda i, c: (i, c, 0)),
                   per_b(c0), per_b(n0), per_b(m0)],
        out_shape=[jax.ShapeDtypeStruct((b, s_len, d_b), BF16),
                   jax.ShapeDtypeStruct(c0.shape, F32),
                   jax.ShapeDtypeStruct(n0.shape, F32),
                   jax.ShapeDtypeStruct(m0.shape, F32)],
        scratch_shapes=[pltpu.VMEM((blk + SUBLANES, 2 * d_b), F32),
                        pltpu.VMEM((blk, d_b), BF16),
                        pltpu.VMEM((blk, d_b), F32),
                        pltpu.VMEM((n_heads, STATE_ROWS, HEAD_DIM_B), F32),
                        pltpu.VMEM((1, LANES), F32)],
        compiler_params=_params("parallel", "arbitrary"),
        name="mlstm",
    )(z3, z3, z3, z3, gates3, conv0, c0, n0, m0, w_conv, b_conv, gate_bias, g_norm)


def _layer_norm(r, g, b):
    mu = jnp.mean(r, axis=-1, keepdims=True)
    rc = r - mu
    var = jnp.mean(rc * rc, axis=-1, keepdims=True)
    return rc * lax.rsqrt(var + 1e-5) * g + b


def _row_splits(n_rows):
    sub = n_rows // LN_SPLITS if n_rows % (LN_SPLITS * 2 * SUBLANES) == 0 else n_rows
    return [slice(r, r + sub) for r in range(0, n_rows, sub)]


def _out_proj_kernel(a_ref, m_ref, x_ref, wa_ref, wm_ref, g_ref, b_ref, x1_ref, x1b_ref, *, alpha):
    for rows in _row_splits(x_ref.shape[0]):
        mix = _mm(a_ref[rows, :], wa_ref[...]) + _mm(m_ref[rows, :], wm_ref[...])
        y = _layer_norm(alpha * x_ref[rows, :] + mix, g_ref[...], b_ref[...])
        x1_ref[rows, :] = y
        x1b_ref[rows, :] = y.astype(BF16)


def _out_proj(attn2d, mlstm2d, x2d, w_out, ln_g, ln_b, alpha, bm):
    t, d = x2d.shape
    da, db = attn2d.shape[1], mlstm2d.shape[1]
    assert da == db
    row = lambda w: pl.BlockSpec((bm, w), lambda i: (i, 0))
    vec = pl.BlockSpec((1, d), lambda i: (0, 0))
    return pl.pallas_call(
        functools.partial(_out_proj_kernel, alpha=alpha),
        grid=(t // bm,),
        in_specs=[row(da), row(db), row(d),
                  pl.BlockSpec((da, d), lambda i: (0, 0)),
                  pl.BlockSpec((db, d), lambda i: (1, 0)),
                  vec, vec],
        out_specs=[row(d), row(d)],
        out_shape=[jax.ShapeDtypeStruct((t, d), F32), jax.ShapeDtypeStruct((t, d), BF16)],
        compiler_params=_params("parallel"),
        name="out_proj_ln",
    )(attn2d, mlstm2d, x2d, w_out, w_out, ln_g, ln_b)


def _ffn_up_kernel(x_ref, wg_ref, wu_ref, h_ref):
    x = x_ref[...]
    gate = _mm(x, wg_ref[...])
    h_ref[...] = (gate * jax.nn.sigmoid(gate) * _mm(x, wu_ref[...])).astype(h_ref.dtype)


def _ffn_up(x1b, w_gate, w_up, bm, bf):
    t, d = x1b.shape
    f = w_gate.shape[1]
    wspec = pl.BlockSpec((d, bf), lambda i, j: (0, j))
    return pl.pallas_call(
        _ffn_up_kernel,
        grid=(t // bm, f // bf),
        in_specs=[pl.BlockSpec((bm, d), lambda i, j: (i, 0)), wspec, wspec],
        out_specs=pl.BlockSpec((bm, bf), lambda i, j: (i, j)),
        out_shape=jax.ShapeDtypeStruct((t, f), BF16),
        compiler_params=_params("parallel", "arbitrary"),
        name="ffn_up",
    )(x1b, w_gate, w_up)


def _ffn_down_kernel(h_ref, wd_ref, x1_ref, g_ref, b_ref, y_ref, *, alpha):
    for rows in _row_splits(x1_ref.shape[0]):
        y_ref[rows, :] = _layer_norm(alpha * x1_ref[rows, :] + _mm(h_ref[rows, :], wd_ref[...]),
                                     g_ref[...], b_ref[...])


def _ffn_down(h, w_down, x1, ln_g, ln_b, alpha, bm):
    t, f = h.shape
    d = w_down.shape[1]
    vec = pl.BlockSpec((1, d), lambda i: (0, 0))
    return pl.pallas_call(
        functools.partial(_ffn_down_kernel, alpha=alpha),
        grid=(t // bm,),
        in_specs=[pl.BlockSpec((bm, f), lambda i: (i, 0)),
                  pl.BlockSpec((f, d), lambda i: (0, 0), pipeline_mode=pl.Buffered(1)),
                  pl.BlockSpec((bm, d), lambda i: (i, 0)),
                  vec, vec],
        out_specs=pl.BlockSpec((bm, d), lambda i: (i, 0)),
        out_shape=jax.ShapeDtypeStruct((t, d), F32),
        compiler_params=_params("parallel"),
        name="ffn_down_ln",
    )(h, w_down, x1, ln_g, ln_b)


def _largest_divisor(n, cap):
    return max(d for d in range(1, min(n, cap) + 1) if n % d == 0)


def _trunk_layer(x, hist_k, hist_v, conv_state, mlstm_state, w, alpha):
    b, s_len, d = x.shape
    t = b * s_len
    c0, n0, m0 = mlstm_state
    n_heads_b = c0.shape[1]
    d_b = n_heads_b * HEAD_DIM_B
    d_attn = w["g_attn"].shape[1]
    n_main = 3 * d_attn + 4 * d_b
    assert d_attn == d_b and w["w_main"].shape[1] == n_main

    bm = _largest_divisor(t, 1024)
    x2d = x.reshape(t, d)
    z, gates = _in_proj(x2d, w["w_main"], w["w_in_f32"], n_heads_b, bm,
                        _largest_divisor(n_main // LANES, 14) * LANES)
    z3 = z.reshape(b, s_len, -1)

    attn, k_tail, v_tail = _attention(z3, hist_k, hist_v, w["bias_pairs"], w["g_attn"])

    conv0 = jnp.pad(conv_state.astype(F32), ((0, 0), (SUBLANES - (CONV_W - 1), 0), (0, 0)))
    m0_row = jnp.pad(m0.astype(F32)[:, None, :], ((0, 0), (0, 0), (0, LANES - n_heads_b)))
    gates3 = gates.reshape(b, s_len, 2 * LANES)
    z3m = z3
    s_pad = -(-s_len // MLSTM_BLK) * MLSTM_BLK
    if s_pad != s_len:
        z3m = jnp.pad(z3, ((0, 0), (0, s_pad - s_len), (0, 0)))
        gates3 = jnp.pad(gates3, ((0, 0), (0, s_pad - s_len), (0, 0)))
    mlstm, c_new, n_new, m_new = _mlstm(
        z3m, gates3, conv0, c0.astype(F32), n0.astype(F32), m0_row,
        w["w_conv"], w["b_conv"], w["gate_bias"], w["g_mlstm"], 3,
        MLSTM_BLK if s_pad == s_len else s_len)
    mlstm = mlstm[:, :s_len]

    bm2 = _largest_divisor(t, 512)
    x1, x1b = _out_proj(attn.reshape(t, d_attn), mlstm.reshape(t, d_b), x2d, w["w_out"],
                        w["ln1_g"], w["ln1_b"], alpha, bm2)
    d_ff = w["w_ffn_gate"].shape[1]
    hmid = _ffn_up(x1b, w["w_ffn_gate"], w["w_ffn_up"], bm, _largest_divisor(d_ff // (2 * LANES), 2) * 2 * LANES)
    y = _ffn_down(hmid, w["w_ffn_down"], x1, w["ln2_g"], w["ln2_b"], alpha, bm2)

    n_heads_a = d_attn // HEAD_DIM_A
    to_heads = lambda a: a.reshape(a.shape[0], a.shape[1], n_heads_a, HEAD_DIM_A)
    qk_raw_tail = z3[:, s_len - (CONV_W - 1):, 3 * d_attn:3 * d_attn + 2 * d_b].astype(F32)
    return (y.reshape(b, s_len, d), to_heads(k_tail), to_heads(v_tail), qk_raw_tail,
            c_new, n_new, m_new[:, 0, :n_heads_b])


def kernel(x_prompt, x_sample, cache_attn_k, cache_attn_v, state_conv, state_mlstm_C, state_mlstm_n, state_mlstm_m, w_in, b_igate, b_fgate, w_conv, b_conv, rel_bias, g_attn_norm, g_mlstm_norm, w_out, ln1_g, ln1_b, w_ffn_gate, w_ffn_up, w_ffn_down, ln2_g, ln2_b):
    depth = w_in.shape[0]
    alpha = (2.0 * depth) ** 0.25
    xp, xs = x_prompt, x_sample
    bp, sp, d_model = xp.shape
    n_heads_b = b_igate.shape[1]
    d_b = n_heads_b * HEAD_DIM_B
    d_attn = g_attn_norm.shape[1]
    n_heads_a = d_attn // HEAD_DIM_A
    n_main = 3 * d_attn + 4 * d_b
    new_p, new_s = [], []
    for l in range(depth):
        q_scale = jnp.where(jnp.arange(n_main) < d_attn, (HEAD_DIM_A ** -0.5) * LOG2_E, 1.0).astype(F32)
        lane_pad = lambda a: jnp.pad(a, ((0, 0), (0, LANES - n_heads_b)))
        w = dict(
            w_main=(w_in[l][:, :n_main] * q_scale[None, :]).astype(BF16),
            w_in_f32=w_in[l],
            gate_bias=jnp.concatenate([lane_pad(b_igate[l][None, :]), lane_pad(b_fgate[l][None, :])],
                                      axis=1).astype(F32),
            w_conv=w_conv[l], b_conv=b_conv[l][None, :],
            bias_pairs=_bias_pairs(rel_bias[l]),
            g_attn=g_attn_norm[l][None, :], g_mlstm=g_mlstm_norm[l][None, :],
            w_out=w_out[l].astype(BF16), ln1_g=ln1_g[l][None, :], ln1_b=ln1_b[l][None, :],
            w_ffn_gate=w_ffn_gate[l].astype(BF16), w_ffn_up=w_ffn_up[l].astype(BF16),
            w_ffn_down=w_ffn_down[l].astype(BF16), ln2_g=ln2_g[l][None, :], ln2_b=ln2_b[l][None, :],
        )
        zero_state = (jnp.zeros((bp, n_heads_b, HEAD_DIM_B, HEAD_DIM_B), F32),
                      jnp.zeros((bp, n_heads_b, HEAD_DIM_B), F32),
                      jnp.zeros((bp, n_heads_b), F32))
        zero_conv = jnp.zeros((bp, CONV_W - 1, 2 * d_b), F32)
        xp, *state_p = _trunk_layer(xp, None, None, zero_conv, zero_state, w, alpha)
        new_p.append(tuple(state_p))
        bs = xs.shape[0]
        ck = cache_attn_k[l].astype(F32).reshape(bs, -1, d_attn)
        cv = cache_attn_v[l].astype(F32).reshape(bs, -1, d_attn)
        assert ck.shape[1] == BAND_ROWS
        xs, *state_s = _trunk_layer(xs, ck, cv, state_conv[l],
                                    (state_mlstm_C[l], state_mlstm_n[l], state_mlstm_m[l]), w, alpha)
        new_s.append(tuple(state_s))
    k_p, v_p, conv_p, C_p, n_p, m_p = [jnp.stack(t) for t in zip(*new_p)]
    k_s, v_s, conv_s, C_s, n_s, m_s = [jnp.stack(t) for t in zip(*new_s)]
    return (xp, xs, k_p, v_p, conv_p, C_p, n_p, m_p, k_s, v_s, conv_s, C_s, n_s, m_s)
```

```python
import functools

import jax
import jax.numpy as jnp
from jax import lax
from jax.experimental import pallas as pl
from jax.experimental.pallas import tpu as pltpu

CHUNK = 64
BAND_ROWS = 8 * CHUNK
BAND = BAND_ROWS + CHUNK
REL_CLIP = 256
HEAD_DIM_A = 64
HEAD_DIM_B = 128
CONV_W = 4
LANES = 128
SUBLANES = 8
MXU_COLS = 256
NEG_BIG = -1e30
LOG2_E = 1.4426950408889634
LN_SPLITS = 4
ATTN_CHUNKS_PER_STEP = 2
VMEM_LIMIT_BYTES = 56 * 1024 * 1024

F32 = jnp.float32
BF16 = jnp.bfloat16


def _params(*semantics, flags=None):
    return pltpu.CompilerParams(dimension_semantics=semantics, vmem_limit_bytes=VMEM_LIMIT_BYTES,
                                flags=flags)


def _mm(a, b):
    return jnp.dot(a, b, preferred_element_type=F32)


def _mm_nt(a, b, precision=None):
    return lax.dot_general(a, b, (((1,), (1,)), ((), ())), precision=precision,
                           preferred_element_type=F32)


def _in_proj_kernel(x_ref, wt_ref, wgt_ref, z_ref, g_ref, xb_ref):
    @pl.when(pl.program_id(1) == 0)
    def _():
        xb = x_ref[...].astype(BF16)
        xb_ref[...] = xb
        g_ref[...] = _mm_nt(xb, wgt_ref[...].astype(BF16))

    z_ref[...] = _mm_nt(xb_ref[...], wt_ref[...]).astype(z_ref.dtype)


def _in_proj(x2d, w_main_t, w_gate_t, bm, bn):
    t, d = x2d.shape
    n = w_main_t.shape[0]
    ng = w_gate_t.shape[0]
    return pl.pallas_call(
        _in_proj_kernel,
        grid=(t // bm, n // bn),
        in_specs=[pl.BlockSpec((bm, d), lambda i, j: (i, 0)),
                  pl.BlockSpec((bn, d), lambda i, j: (j, 0)),
                  pl.BlockSpec((ng, d), lambda i, j: (0, 0))],
        out_specs=[pl.BlockSpec((bm, bn), lambda i, j: (i, j)),
                   pl.BlockSpec((bm, ng), lambda i, j: (i, 0))],
        out_shape=[jax.ShapeDtypeStruct((t, n), BF16),
                   jax.ShapeDtypeStruct((t, ng), F32)],
        scratch_shapes=[pltpu.VMEM((bm, d), BF16)],
        compiler_params=_params("parallel", "arbitrary"),
        name="in_proj",
    )(x2d, w_main_t, w_gate_t)


def _attn_kernel(*refs, has_hist, n_pairs, n_sub):
    if has_hist:
        q_ref, k_ref, v_ref, kh_ref, vh_ref, bias_ref, g_ref, o_ref, kt_ref, vt_ref, kpad, vpad = refs
    else:
        q_ref, k_ref, v_ref, bias_ref, g_ref, o_ref, kt_ref, vt_ref, kpad, vpad = refs
    step = pl.program_id(1)
    s_len = k_ref.shape[1]
    n_keep = kt_ref.shape[1]
    n_new = min(n_keep, s_len)

    @pl.when(step == 0)
    def _():
        if has_hist:
            kpad[0:BAND_ROWS, :] = kh_ref[0].astype(BF16)
            vpad[0:BAND_ROWS, :] = vh_ref[0].astype(BF16)
            if n_new < n_keep:
                kt_ref[0, 0:n_keep - n_new, :] = kh_ref[0, BAND_ROWS - (n_keep - n_new):, :]
                vt_ref[0, 0:n_keep - n_new, :] = vh_ref[0, BAND_ROWS - (n_keep - n_new):, :]
        else:
            kpad[0:BAND_ROWS, :] = jnp.zeros((BAND_ROWS, kpad.shape[1]), BF16)
            vpad[0:BAND_ROWS, :] = jnp.zeros((BAND_ROWS, vpad.shape[1]), BF16)
        kpad[BAND_ROWS:BAND_ROWS + s_len, :] = k_ref[0]
        vpad[BAND_ROWS:BAND_ROWS + s_len, :] = v_ref[0]
        kt_ref[0, n_keep - n_new:, :] = k_ref[0, s_len - n_new:, :].astype(F32)
        vt_ref[0, n_keep - n_new:, :] = v_ref[0, s_len - n_new:, :].astype(F32)

    def chunks(masked):
        for u in range(n_sub):
            _attn_chunk(q_ref, kpad, vpad, bias_ref, g_ref, o_ref, step * n_sub + u, u, n_pairs, masked)

    if has_hist:
        chunks(masked=False)
    else:
        first_steps = BAND_ROWS // CHUNK // n_sub
        pl.when(step < first_steps)(functools.partial(chunks, masked=True))
        pl.when(step >= first_steps)(functools.partial(chunks, masked=False))


def _attn_chunk(q_ref, kpad, vpad, bias_ref, g_ref, o_ref, c, u, n_pairs, masked):
    start = pl.multiple_of(c * CHUNK, CHUNK)
    rows = slice(u * CHUNK, (u + 1) * CHUNK)
    lane = lax.broadcasted_iota(jnp.int32, (CHUNK, LANES), 1)
    row2 = lax.broadcasted_iota(jnp.int32, (2 * CHUNK, LANES), 0)
    lane2 = lax.broadcasted_iota(jnp.int32, (2 * CHUNK, LANES), 1)
    own_head = (row2 >= CHUNK) == (lane2 >= HEAD_DIM_A)
    if masked:
        col = lax.broadcasted_iota(jnp.int32, (2 * CHUNK, BAND), 1)
        valid = col >= BAND_ROWS - c * CHUNK

    def scores(p):
        cols = slice(p * LANES, (p + 1) * LANES)
        qp = q_ref[0, rows, cols]
        zero = jnp.zeros_like(qp)
        q2 = jnp.concatenate([jnp.where(lane < HEAD_DIM_A, qp, zero),
                              jnp.where(lane >= HEAD_DIM_A, qp, zero)], axis=0)
        return _mm_nt(q2, kpad[pl.ds(start, BAND), cols])

    pairs = range(n_pairs)
    s = [scores(p) + bias_ref[p] for p in pairs]
    if masked:
        s = [jnp.where(valid, s[p], NEG_BIG) for p in pairs]
    m = [jnp.max(s[p], axis=-1, keepdims=True) for p in pairs]
    e = [jnp.exp2((s[p] - m[p]).astype(BF16)) for p in pairs]
    ones = jnp.ones((BAND, LANES), BF16)
    o = [_mm(e[p], jnp.concatenate([vpad[pl.ds(start, BAND), p * LANES:(p + 1) * LANES], ones], axis=1))
         for p in pairs]
    o = [jnp.where(own_head, o[p][:, :LANES] / o[p][:, LANES:], 0.0) for p in pairs]
    ssq = [jnp.sum(o[p] * o[p], axis=-1, keepdims=True) for p in pairs]
    for p in pairs:
        cols = slice(p * LANES, (p + 1) * LANES)
        on = o[p] * lax.rsqrt(ssq[p] * (1.0 / HEAD_DIM_A) + 1e-6)
        o_ref[0, rows, cols] = ((on[:CHUNK] + on[CHUNK:]) * g_ref[:, cols]).astype(o_ref.dtype)


def _attention(z3, hist_k, hist_v, bias_pairs, g_attn):
    b, s_len, _ = z3.shape
    d_attn = g_attn.shape[1]
    n_pairs = d_attn // LANES
    has_hist = hist_k is not None
    n_keep = BAND_ROWS if has_hist else min(BAND_ROWS, s_len)
    n_chunks = s_len // CHUNK
    n_sub = ATTN_CHUNKS_PER_STEP if n_chunks % ATTN_CHUNKS_PER_STEP == 0 else 1
    assert (BAND_ROWS // CHUNK) % n_sub == 0
    in_specs = [pl.BlockSpec((1, n_sub * CHUNK, d_attn), lambda i, c: (i, c, 0)),
                pl.BlockSpec((1, s_len, d_attn), lambda i, c: (i, 0, 1)),
                pl.BlockSpec((1, s_len, d_attn), lambda i, c: (i, 0, 2))]
    args = [z3, z3, z3]
    if has_hist:
        in_specs += [pl.BlockSpec((1, BAND_ROWS, d_attn), lambda i, c: (i, 0, 0))] * 2
        args += [hist_k, hist_v]
    in_specs += [pl.BlockSpec(bias_pairs.shape, lambda i, c: (0, 0, 0)),
                 pl.BlockSpec((1, d_attn), lambda i, c: (0, 0))]
    args += [bias_pairs, g_attn]
    tail_spec = pl.BlockSpec((1, n_keep, d_attn), lambda i, c: (i, 0, 0))
    tail_shape = jax.ShapeDtypeStruct((b, n_keep, d_attn), F32)
    return pl.pallas_call(
        functools.partial(_attn_kernel, has_hist=has_hist, n_pairs=n_pairs, n_sub=n_sub),
        grid=(b, n_chunks // n_sub),
        in_specs=in_specs,
        out_specs=[pl.BlockSpec((1, n_sub * CHUNK, d_attn), lambda i, c: (i, c, 0)), tail_spec, tail_spec],
        out_shape=[jax.ShapeDtypeStruct((b, s_len, d_attn), BF16), tail_shape, tail_shape],
        scratch_shapes=[pltpu.VMEM((BAND_ROWS + s_len, d_attn), BF16)] * 2,
        compiler_params=_params("parallel", "arbitrary"),
        name="attn_hist" if has_hist else "attn",
    )(*args)


def _bias_pairs(rel_bias):
    n_heads = rel_bias.shape[0]
    n_far = BAND - REL_CLIP
    far = jnp.broadcast_to(rel_bias[:, 2 * REL_CLIP:], (n_heads, n_far))
    near = rel_bias[:, REL_CLIP - (CHUNK - 1):2 * REL_CLIP][:, ::-1]
    ext = jnp.concatenate([far, near], axis=1)
    n_ext = BAND + CHUNK - 1
    flat = jnp.tile(jnp.pad(ext, ((0, 0), (0, 1))), (1, CHUNK))
    bias = flat[:, CHUNK - 1:CHUNK - 1 + CHUNK * n_ext].reshape(n_heads, CHUNK, n_ext)[:, :, :BAND]
    return (bias * LOG2_E).reshape(n_heads // 2, 2 * CHUNK, BAND).astype(F32)


MLSTM_BLK = 128
STATE_ROWS = HEAD_DIM_B + 16


def _sigmoid(x):
    return 0.5 * jnp.tanh(0.5 * x) + 0.5


def _log_sigmoid(x):
    return jnp.minimum(x, 0.0) - jnp.log1p(jnp.exp(-jnp.abs(x)))


def _mm_tn(a, b):
    return lax.dot_general(a, b, (((0,), (0,)), ((), ())), preferred_element_type=F32)


def _mlstm_kernel(q_ref, k_ref, v_ref, og_ref, gates_ref, conv0_ref, c0_ref, n0_ref, m0_ref,
                  wconv_ref, bconv_ref, gbias_ref, gnorm_ref,
                  o_ref, c_out_ref, n_out_ref, m_out_ref,
                  xs, qb_s, k_s, ct_s, m_s, *, n_heads, valid_len):
    blk = MLSTM_BLK
    hd = HEAD_DIM_B
    c = pl.program_id(1)
    last = pl.num_programs(1) - 1
    d_b = n_heads * hd
    pad = SUBLANES
    heads = range(n_heads)

    @pl.when(c == 0)
    def _():
        xs[0:pad, :] = conv0_ref[0]
        for h in heads:
            ct_s[h, 0:hd, :] = c0_ref[0, h].T
            ct_s[h, hd:, :] = jnp.zeros((STATE_ROWS - hd, hd), F32)
            ct_s[h, hd:hd + 1, :] = n0_ref[0, h:h + 1, :]
        m_s[...] = m0_ref[0]

    xs[pad:pad + blk, 0:d_b] = q_ref[0].astype(F32)
    xs[pad:pad + blk, d_b:2 * d_b] = k_ref[0].astype(F32)

    gates = gates_ref[0] + gbias_ref[...]
    ig = gates[:, :LANES]
    lf = _log_sigmoid(gates[:, LANES:])
    if valid_len < blk:
        live = lax.broadcasted_iota(jnp.int32, (blk, LANES), 0) < valid_len
        ig = jnp.where(live, ig, NEG_BIG)
        lf = jnp.where(live, lf, 0.0)
    row_i = lax.broadcasted_iota(jnp.int32, (blk, blk), 0)
    col_i = lax.broadcasted_iota(jnp.int32, (blk, blk), 1)
    f_cum = jnp.dot(jnp.where(row_i >= col_i, 1.0, 0.0).astype(F32), lf,
                    precision=lax.Precision.HIGHEST, preferred_element_type=F32)
    sel_rows = -(-n_heads // SUBLANES) * SUBLANES
    sel = jnp.where(lax.broadcasted_iota(jnp.int32, (sel_rows, LANES), 0)
                    == lax.broadcasted_iota(jnp.int32, (sel_rows, LANES), 1), 1.0, 0.0).astype(F32)
    f_rows = _mm_nt(sel, f_cum, precision=lax.Precision.HIGHEST)
    imf = ig - f_cum
    m_prev = m_s[...]
    f_last = f_cum[blk - 1:blk, :]
    g = f_last + imf
    a_last = f_last + m_prev
    m_new = jnp.maximum(a_last, jnp.max(g, axis=0, keepdims=True))
    w_old = jnp.exp(a_last - m_new)
    wk = jnp.exp(g - m_new)

    for j in range(2 * n_heads):
        cols = slice(j * hd, (j + 1) * hd)
        acc = bconv_ref[:, cols]
        for tap in range(CONV_W):
            acc = acc + xs[pl.ds(pad - (CONV_W - 1) + tap, blk), cols] * wconv_ref[tap:tap + 1, cols]
        y = acc * _sigmoid(acc)
        if j < n_heads:
            qb_s[:, cols] = y.astype(BF16)
        else:
            k_s[:, (j - n_heads) * hd:(j - n_heads + 1) * hd] = y * (hd ** -0.5)

    hcols = [slice(h * hd, (h + 1) * hd) for h in heads]
    st = [_mm_nt(k_s[:, hcols[h]].astype(BF16), qb_s[:, hcols[h]]) for h in heads]
    inter = [_mm_nt(ct_s[h].astype(BF16), qb_s[:, hcols[h]]) for h in heads]

    visible = row_i <= col_i
    f_row = [f_rows[h:h + 1, :] for h in heads]
    a_row = [f_row[h] + m_prev[:, h:h + 1] for h in heads]
    dt = [jnp.where(visible, jnp.broadcast_to(imf[:, h:h + 1], (blk, blk)) + f_row[h], NEG_BIG) for h in heads]
    m_t = [jnp.maximum(a_row[h], jnp.max(dt[h], axis=0, keepdims=True)) for h in heads]
    pt = [st[h] * jnp.exp(dt[h] - m_t[h]) for h in heads]
    w_inter = [jnp.exp(a_row[h] - m_t[h]) for h in heads]
    den = [jnp.sum(pt[h], axis=0, keepdims=True) + w_inter[h] * inter[h][hd:hd + 1, :] for h in heads]
    scale = [1.0 / jnp.maximum(jnp.abs(den[h]), jnp.exp(-m_t[h])) for h in heads]
    kw = [k_s[:, hcols[h]] * jnp.broadcast_to(wk[:, h:h + 1], (blk, hd)) for h in heads]

    num_t = [_mm_tn(v_ref[0, :, hcols[h]], pt[h].astype(BF16)) for h in heads]
    dct = [_mm_tn(v_ref[0, :, hcols[h]], kw[h].astype(BF16)) for h in heads]

    ht = [(num_t[h] + w_inter[h] * inter[h][0:hd, :]) * scale[h] for h in heads]
    ssq = [jnp.sum(ht[h] * ht[h], axis=0, keepdims=True) for h in heads]
    for h in heads:
        hn = (ht[h] * lax.rsqrt(ssq[h] * (1.0 / hd) + 1e-6)).T
        out = hn * gnorm_ref[:, hcols[h]] * _sigmoid(og_ref[0, :, hcols[h]].astype(F32))
        o_ref[0, :, hcols[h]] = out.astype(o_ref.dtype)
        decay = w_old[:, h:h + 1]
        ct_s[h, 0:hd, :] = decay * ct_s[h, 0:hd, :] + dct[h]
        ct_s[h, hd:hd + 1, :] = decay * ct_s[h, hd:hd + 1, :] + jnp.sum(kw[h], axis=0, keepdims=True)
    m_s[...] = m_new
    xs[0:pad, :] = xs[blk:blk + pad, :]

    @pl.when(c == last)
    def _():
        for h in heads:
            c_out_ref[0, h] = ct_s[h, 0:hd, :].T
            n_out_ref[0, h:h + 1, :] = ct_s[h, hd:hd + 1, :]
        m_out_ref[0] = m_s[...]


def _mlstm(z3, gates3, conv0, c0, n0, m0, w_conv, b_conv, gate_bias, g_norm, col_block0, valid_len):
    b, s_len, _ = z3.shape
    n_heads = c0.shape[1]
    d_b = n_heads * HEAD_DIM_B
    blk = MLSTM_BLK
    assert s_len % blk == 0 and n_heads <= LANES and (valid_len == blk or s_len == blk)
    zspec = lambda k: pl.BlockSpec((1, blk, d_b), lambda i, c: (i, c, col_block0 + k))
    full = lambda a: pl.BlockSpec(a.shape, lambda i, c: (0,) * a.ndim)
    per_b = lambda a: pl.BlockSpec((1,) + a.shape[1:], lambda i, c: (i,) + (0,) * (a.ndim - 1))
    return pl.pallas_call(
        functools.partial(_mlstm_kernel, n_heads=n_heads, valid_len=valid_len),
        grid=(b, s_len // blk),
        in_specs=[zspec(0), zspec(1), zspec(2), zspec(3),
                  pl.BlockSpec((1, blk, 2 * LANES), lambda i, c: (i, c, 0)),
                  per_b(conv0), per_b(c0), per_b(n0), per_b(m0),
                  full(w_conv), full(b_conv), full(gate_bias), full(g_norm)],
        out_specs=[pl.BlockSpec((1, blk, d_b), lambda i, c: (i, c, 0)),
                   per_b(c0), per_b(n0), per_b(m0)],
        out_shape=[jax.ShapeDtypeStruct((b, s_len, d_b), BF16),
                   jax.ShapeDtypeStruct(c0.shape, F32),
                   jax.ShapeDtypeStruct(n0.shape, F32),
                   jax.ShapeDtypeStruct(m0.shape, F32)],
        scratch_shapes=[pltpu.VMEM((blk + SUBLANES, 2 * d_b), F32),
                        pltpu.VMEM((blk, d_b), BF16),
                        pltpu.VMEM((blk, d_b), F32),
                        pltpu.VMEM((n_heads, STATE_ROWS, HEAD_DIM_B), F32),
                        pltpu.VMEM((1, LANES), F32)],
        compiler_params=_params("parallel", "arbitrary"),
        name="mlstm",
    )(z3, z3, z3, z3, gates3, conv0, c0, n0, m0, w_conv, b_conv, gate_bias, g_norm)


def _layer_norm(r, g, b):
    mu = jnp.mean(r, axis=-1, keepdims=True)
    rc = r - mu
    var = jnp.mean(rc * rc, axis=-1, keepdims=True)
    return rc * lax.rsqrt(var + 1e-5) * g + b


def _row_splits(n_rows):
    sub = n_rows // LN_SPLITS if n_rows % (LN_SPLITS * 2 * SUBLANES) == 0 else n_rows
    return [slice(r, r + sub) for r in range(0, n_rows, sub)]


def _out_proj_kernel(a_ref, m_ref, x_ref, wa_ref, wm_ref, g_ref, b_ref, x1_ref, x1b_ref, *, alpha):
    for rows in _row_splits(x_ref.shape[0]):
        mix = _mm(a_ref[rows, :], wa_ref[...]) + _mm(m_ref[rows, :], wm_ref[...])
        y = _layer_norm(alpha * x_ref[rows, :] + mix, g_ref[...], b_ref[...])
        x1_ref[rows, :] = y
        x1b_ref[rows, :] = y.astype(BF16)


def _out_proj(attn2d, mlstm2d, x2d, w_out, ln_g, ln_b, alpha, bm):
    t, d = x2d.shape
    da, db = attn2d.shape[1], mlstm2d.shape[1]
    assert da == db
    row = lambda w: pl.BlockSpec((bm, w), lambda i: (i, 0))
    vec = pl.BlockSpec((1, d), lambda i: (0, 0))
    return pl.pallas_call(
        functools.partial(_out_proj_kernel, alpha=alpha),
        grid=(t // bm,),
        in_specs=[row(da), row(db), row(d),
                  pl.BlockSpec((da, d), lambda i: (0, 0)),
                  pl.BlockSpec((db, d), lambda i: (1, 0)),
                  vec, vec],
        out_specs=[row(d), row(d)],
        out_shape=[jax.ShapeDtypeStruct((t, d), F32), jax.ShapeDtypeStruct((t, d), BF16)],
        compiler_params=_params("parallel"),
        name="out_proj_ln",
    )(attn2d, mlstm2d, x2d, w_out, w_out, ln_g, ln_b)


def _ffn_up_kernel(x_ref, wg_ref, wu_ref, h_ref):
    x = x_ref[...]
    gate = _mm(x, wg_ref[...])
    h_ref[...] = (gate * jax.nn.sigmoid(gate) * _mm(x, wu_ref[...])).astype(h_ref.dtype)


def _ffn_up(x1b, w_gate, w_up, bm, bf):
    t, d = x1b.shape
    f = w_gate.shape[1]
    wspec = pl.BlockSpec((d, bf), lambda i, j: (0, j))
    return pl.pallas_call(
        _ffn_up_kernel,
        grid=(t // bm, f // bf),
        in_specs=[pl.BlockSpec((bm, d), lambda i, j: (i, 0)), wspec, wspec],
        out_specs=pl.BlockSpec((bm, bf), lambda i, j: (i, j)),
        out_shape=jax.ShapeDtypeStruct((t, f), BF16),
        compiler_params=_params("parallel", "arbitrary"),
        name="ffn_up",
    )(x1b, w_gate, w_up)


def _ffn_down_kernel(h_ref, wd_ref, x1_ref, g_ref, b_ref, y_ref, *, alpha):
    for rows in _row_splits(x1_ref.shape[0]):
        y_ref[rows, :] = _layer_norm(alpha * x1_ref[rows, :] + _mm(h_ref[rows, :], wd_ref[...]),
                                     g_ref[...], b_ref[...])


def _ffn_down(h, w_down, x1, ln_g, ln_b, alpha, bm):
    t, f = h.shape
    d = w_down.shape[1]
    vec = pl.BlockSpec((1, d), lambda i: (0, 0))
    return pl.pallas_call(
        functools.partial(_ffn_down_kernel, alpha=alpha),
        grid=(t // bm,),
        in_specs=[pl.BlockSpec((bm, f), lambda i: (i, 0)),
                  pl.BlockSpec((f, d), lambda i: (0, 0), pipeline_mode=pl.Buffered(1)),
                  pl.BlockSpec((bm, d), lambda i: (i, 0)),
                  vec, vec],
        out_specs=pl.BlockSpec((bm, d), lambda i: (i, 0)),
        out_shape=jax.ShapeDtypeStruct((t, d), F32),
        compiler_params=_params("parallel"),
        name="ffn_down_ln",
    )(h, w_down, x1, ln_g, ln_b)


def _largest_divisor(n, cap):
    return max(d for d in range(1, min(n, cap) + 1) if n % d == 0)


def _trunk_layer(x, hist_k, hist_v, conv_state, mlstm_state, w, alpha):
    b, s_len, d = x.shape
    t = b * s_len
    c0, n0, m0 = mlstm_state
    n_heads_b = c0.shape[1]
    d_b = n_heads_b * HEAD_DIM_B
    d_attn = w["g_attn"].shape[1]
    n_main = 3 * d_attn + 4 * d_b
    assert d_attn == d_b and w["w_main_t"].shape[0] == n_main

    bm = _largest_divisor(t, 1024)
    x2d = x.reshape(t, d)
    z, gates = _in_proj(x2d, w["w_main_t"], w["w_gate_t"], bm, _largest_divisor(n_main // LANES, 14) * LANES)
    z3 = z.reshape(b, s_len, -1)

    attn, k_tail, v_tail = _attention(z3, hist_k, hist_v, w["bias_pairs"], w["g_attn"])

    conv0 = jnp.pad(conv_state.astype(F32), ((0, 0), (SUBLANES - (CONV_W - 1), 0), (0, 0)))
    m0_row = jnp.pad(m0.astype(F32)[:, None, :], ((0, 0), (0, 0), (0, LANES - n_heads_b)))
    gates3 = gates.reshape(b, s_len, 2 * LANES)
    z3m = z3
    s_pad = -(-s_len // MLSTM_BLK) * MLSTM_BLK
    if s_pad != s_len:
        z3m = jnp.pad(z3, ((0, 0), (0, s_pad - s_len), (0, 0)))
        gates3 = jnp.pad(gates3, ((0, 0), (0, s_pad - s_len), (0, 0)))
    mlstm, c_new, n_new, m_new = _mlstm(
        z3m, gates3, conv0, c0.astype(F32), n0.astype(F32), m0_row,
        w["w_conv"], w["b_conv"], w["gate_bias"], w["g_mlstm"], 3,
        MLSTM_BLK if s_pad == s_len else s_len)
    mlstm = mlstm[:, :s_len]

    bm2 = _largest_divisor(t, 512)
    x1, x1b = _out_proj(attn.reshape(t, d_attn), mlstm.reshape(t, d_b), x2d, w["w_out"],
                        w["ln1_g"], w["ln1_b"], alpha, bm2)
    d_ff = w["w_ffn_gate"].shape[1]
    hmid = _ffn_up(x1b, w["w_ffn_gate"], w["w_ffn_up"], bm, _largest_divisor(d_ff // MXU_COLS, 2) * MXU_COLS)
    y = _ffn_down(hmid, w["w_ffn_down"], x1, w["ln2_g"], w["ln2_b"], alpha, bm2)

    n_heads_a = d_attn // HEAD_DIM_A
    to_heads = lambda a: a.reshape(a.shape[0], a.shape[1], n_heads_a, HEAD_DIM_A)
    qk_raw_tail = z3[:, s_len - (CONV_W - 1):, 3 * d_attn:3 * d_attn + 2 * d_b].astype(F32)
    return (y.reshape(b, s_len, d), to_heads(k_tail), to_heads(v_tail), qk_raw_tail,
            c_new, n_new, m_new[:, 0, :n_heads_b])


def kernel(x_prompt, x_sample, cache_attn_k, cache_attn_v, state_conv, state_mlstm_C, state_mlstm_n, state_mlstm_m, w_in, b_igate, b_fgate, w_conv, b_conv, rel_bias, g_attn_norm, g_mlstm_norm, w_out, ln1_g, ln1_b, w_ffn_gate, w_ffn_up, w_ffn_down, ln2_g, ln2_b):
    depth = w_in.shape[0]
    alpha = (2.0 * depth) ** 0.25
    xp, xs = x_prompt, x_sample
    bp, sp, d_model = xp.shape
    n_heads_b = b_igate.shape[1]
    d_b = n_heads_b * HEAD_DIM_B
    d_attn = g_attn_norm.shape[1]
    n_heads_a = d_attn // HEAD_DIM_A
    n_main = 3 * d_attn + 4 * d_b
    new_p, new_s = [], []
    for l in range(depth):
        q_scale = jnp.where(jnp.arange(n_main) < d_attn, (HEAD_DIM_A ** -0.5) * LOG2_E, 1.0).astype(F32)
        w_in_t = jnp.swapaxes(w_in[l], 0, 1)
        lane_pad = lambda a: jnp.pad(a, ((0, 0), (0, LANES - n_heads_b)))
        row_pad = lambda a: jnp.pad(a, ((0, LANES - n_heads_b), (0, 0)))
        w = dict(
            w_main_t=(w_in_t[:n_main] * q_scale[:, None]).astype(BF16),
            w_gate_t=jnp.concatenate([row_pad(w_in_t[n_main:n_main + n_heads_b]),
                                      row_pad(w_in_t[n_main + n_heads_b:])], axis=0),
            gate_bias=jnp.concatenate([lane_pad(b_igate[l][None, :]), lane_pad(b_fgate[l][None, :])],
                                      axis=1).astype(F32),
            w_conv=w_conv[l], b_conv=b_conv[l][None, :],
            bias_pairs=_bias_pairs(rel_bias[l]),
            g_attn=g_attn_norm[l][None, :], g_mlstm=g_mlstm_norm[l][None, :],
            w_out=w_out[l].astype(BF16), ln1_g=ln1_g[l][None, :], ln1_b=ln1_b[l][None, :],
            w_ffn_gate=w_ffn_gate[l].astype(BF16), w_ffn_up=w_ffn_up[l].astype(BF16),
            w_ffn_down=w_ffn_down[l].astype(BF16), ln2_g=ln2_g[l][None, :], ln2_b=ln2_b[l][None, :],
        )
        zero_state = (jnp.zeros((bp, n_heads_b, HEAD_DIM_B, HEAD_DIM_B), F32),
                      jnp.zeros((bp, n_heads_b, HEAD_DIM_B), F32),
                      jnp.zeros((bp, n_heads_b), F32))
        zero_conv = jnp.zeros((bp, CONV_W - 1, 2 * d_b), F32)
        xp, *state_p = _trunk_layer(xp, None, None, zero_conv, zero_state, w, alpha)
        new_p.append(tuple(state_p))
        bs = xs.shape[0]
        ck = cache_attn_k[l].astype(F32).reshape(bs, -1, d_attn)
        cv = cache_attn_v[l].astype(F32).reshape(bs, -1, d_attn)
        assert ck.shape[1] == BAND_ROWS
        xs, *state_s = _trunk_layer(xs, ck, cv, state_conv[l],
                                    (state_mlstm_C[l], state_mlstm_n[l], state_mlstm_m[l]), w, alpha)
        new_s.append(tuple(state_s))
    k_p, v_p, conv_p, C_p, n_p, m_p = [jnp.stack(t) for t in zip(*new_p)]
    k_s, v_s, conv_s, C_s, n_s, m_s = [jnp.stack(t) for t in zip(*new_s)]
    return (xp, xs, k_p, v_p, conv_p, C_p, n_p, m_p, k_s, v_s, conv_s, C_s, n_s, m_s)
```

```python
import functools

import jax
import jax.numpy as jnp
from jax import lax
from jax.experimental import pallas as pl
from jax.experimental.pallas import tpu as pltpu

CHUNK = 64
BAND_ROWS = 8 * CHUNK
BAND = BAND_ROWS + CHUNK
REL_CLIP = 256
HEAD_DIM_A = 64
HEAD_DIM_B = 128
CONV_W = 4
LANES = 128
SUBLANES = 8
MXU_COLS = 256
NEG_BIG = -1e30
LOG2_E = 1.4426950408889634
LN_SPLITS = 4
ATTN_CHUNKS_PER_STEP = 4
VMEM_LIMIT_BYTES = 56 * 1024 * 1024

F32 = jnp.float32
BF16 = jnp.bfloat16


def _params(*semantics, flags=None):
    return pltpu.CompilerParams(dimension_semantics=semantics, vmem_limit_bytes=VMEM_LIMIT_BYTES,
                                flags=flags)


def _mm(a, b):
    return jnp.dot(a, b, preferred_element_type=F32)


def _mm_nt(a, b, precision=None):
    return lax.dot_general(a, b, (((1,), (1,)), ((), ())), precision=precision,
                           preferred_element_type=F32)


def _in_proj_kernel(x_ref, wt_ref, wgt_ref, z_ref, g_ref, xb_ref):
    @pl.when(pl.program_id(1) == 0)
    def _():
        xb = x_ref[...].astype(BF16)
        xb_ref[...] = xb
        g_ref[...] = _mm_nt(xb, wgt_ref[...].astype(BF16))

    z_ref[...] = _mm_nt(xb_ref[...], wt_ref[...]).astype(z_ref.dtype)


def _in_proj(x2d, w_main_t, w_gate_t, bm, bn):
    t, d = x2d.shape
    n = w_main_t.shape[0]
    ng = w_gate_t.shape[0]
    return pl.pallas_call(
        _in_proj_kernel,
        grid=(t // bm, n // bn),
        in_specs=[pl.BlockSpec((bm, d), lambda i, j: (i, 0)),
                  pl.BlockSpec((bn, d), lambda i, j: (j, 0)),
                  pl.BlockSpec((ng, d), lambda i, j: (0, 0))],
        out_specs=[pl.BlockSpec((bm, bn), lambda i, j: (i, j)),
                   pl.BlockSpec((bm, ng), lambda i, j: (i, 0))],
        out_shape=[jax.ShapeDtypeStruct((t, n), BF16),
                   jax.ShapeDtypeStruct((t, ng), F32)],
        scratch_shapes=[pltpu.VMEM((bm, d), BF16)],
        compiler_params=_params("parallel", "arbitrary"),
        name="in_proj",
    )(x2d, w_main_t, w_gate_t)


def _attn_kernel(*refs, has_hist, n_pairs, n_sub):
    if has_hist:
        q_ref, k_ref, v_ref, kh_ref, vh_ref, bias_ref, g_ref, o_ref, kt_ref, vt_ref, kpad, vpad = refs
    else:
        q_ref, k_ref, v_ref, bias_ref, g_ref, o_ref, kt_ref, vt_ref, kpad, vpad = refs
    step = pl.program_id(1)
    s_len = k_ref.shape[1]
    n_keep = kt_ref.shape[1]
    n_new = min(n_keep, s_len)

    @pl.when(step == 0)
    def _():
        if has_hist:
            kpad[0:BAND_ROWS, :] = kh_ref[0].astype(BF16)
            vpad[0:BAND_ROWS, :] = vh_ref[0].astype(BF16)
            if n_new < n_keep:
                kt_ref[0, 0:n_keep - n_new, :] = kh_ref[0, BAND_ROWS - (n_keep - n_new):, :]
                vt_ref[0, 0:n_keep - n_new, :] = vh_ref[0, BAND_ROWS - (n_keep - n_new):, :]
        else:
            kpad[0:BAND_ROWS, :] = jnp.zeros((BAND_ROWS, kpad.shape[1]), BF16)
            vpad[0:BAND_ROWS, :] = jnp.zeros((BAND_ROWS, vpad.shape[1]), BF16)
        kpad[BAND_ROWS:BAND_ROWS + s_len, :] = k_ref[0]
        vpad[BAND_ROWS:BAND_ROWS + s_len, :] = v_ref[0]
        kt_ref[0, n_keep - n_new:, :] = k_ref[0, s_len - n_new:, :].astype(F32)
        vt_ref[0, n_keep - n_new:, :] = v_ref[0, s_len - n_new:, :].astype(F32)

    def chunks(masked):
        for u in range(n_sub):
            _attn_chunk(q_ref, kpad, vpad, bias_ref, g_ref, o_ref, step * n_sub + u, u, n_pairs, masked)

    if has_hist:
        chunks(masked=False)
    else:
        first_steps = BAND_ROWS // CHUNK // n_sub
        pl.when(step < first_steps)(functools.partial(chunks, masked=True))
        pl.when(step >= first_steps)(functools.partial(chunks, masked=False))


def _attn_chunk(q_ref, kpad, vpad, bias_ref, g_ref, o_ref, c, u, n_pairs, masked):
    start = pl.multiple_of(c * CHUNK, CHUNK)
    rows = slice(u * CHUNK, (u + 1) * CHUNK)
    lane = lax.broadcasted_iota(jnp.int32, (CHUNK, LANES), 1)
    row2 = lax.broadcasted_iota(jnp.int32, (2 * CHUNK, LANES), 0)
    lane2 = lax.broadcasted_iota(jnp.int32, (2 * CHUNK, LANES), 1)
    own_head = (row2 >= CHUNK) == (lane2 >= HEAD_DIM_A)
    if masked:
        col = lax.broadcasted_iota(jnp.int32, (2 * CHUNK, BAND), 1)
        valid = col >= BAND_ROWS - c * CHUNK

    def scores(p):
        cols = slice(p * LANES, (p + 1) * LANES)
        qp = q_ref[0, rows, cols]
        zero = jnp.zeros_like(qp)
        q2 = jnp.concatenate([jnp.where(lane < HEAD_DIM_A, qp, zero),
                              jnp.where(lane >= HEAD_DIM_A, qp, zero)], axis=0)
        return _mm_nt(q2, kpad[pl.ds(start, BAND), cols])

    pairs = range(n_pairs)
    s = [scores(p) + bias_ref[p] for p in pairs]
    if masked:
        s = [jnp.where(valid, s[p], NEG_BIG) for p in pairs]
    m = [jnp.max(s[p], axis=-1, keepdims=True) for p in pairs]
    e = [jnp.exp2((s[p] - m[p]).astype(BF16)) for p in pairs]
    ones = jnp.ones((BAND, LANES), BF16)
    o = [_mm(e[p], jnp.concatenate([vpad[pl.ds(start, BAND), p * LANES:(p + 1) * LANES], ones], axis=1))
         for p in pairs]
    o = [jnp.where(own_head, o[p][:, :LANES] / o[p][:, LANES:], 0.0) for p in pairs]
    ssq = [jnp.sum(o[p] * o[p], axis=-1, keepdims=True) for p in pairs]
    for p in pairs:
        cols = slice(p * LANES, (p + 1) * LANES)
        on = o[p] * lax.rsqrt(ssq[p] * (1.0 / HEAD_DIM_A) + 1e-6)
        o_ref[0, rows, cols] = ((on[:CHUNK] + on[CHUNK:]) * g_ref[:, cols]).astype(o_ref.dtype)


def _attention(z3, hist_k, hist_v, bias_pairs, g_attn):
    b, s_len, _ = z3.shape
    d_attn = g_attn.shape[1]
    n_pairs = d_attn // LANES
    has_hist = hist_k is not None
    n_keep = BAND_ROWS if has_hist else min(BAND_ROWS, s_len)
    n_chunks = s_len // CHUNK
    n_sub = ATTN_CHUNKS_PER_STEP if n_chunks % ATTN_CHUNKS_PER_STEP == 0 else 1
    assert (BAND_ROWS // CHUNK) % n_sub == 0
    in_specs = [pl.BlockSpec((1, n_sub * CHUNK, d_attn), lambda i, c: (i, c, 0)),
                pl.BlockSpec((1, s_len, d_attn), lambda i, c: (i, 0, 1)),
                pl.BlockSpec((1, s_len, d_attn), lambda i, c: (i, 0, 2))]
    args = [z3, z3, z3]
    if has_hist:
        in_specs += [pl.BlockSpec((1, BAND_ROWS, d_attn), lambda i, c: (i, 0, 0))] * 2
        args += [hist_k, hist_v]
    in_specs += [pl.BlockSpec(bias_pairs.shape, lambda i, c: (0, 0, 0)),
                 pl.BlockSpec((1, d_attn), lambda i, c: (0, 0))]
    args += [bias_pairs, g_attn]
    tail_spec = pl.BlockSpec((1, n_keep, d_attn), lambda i, c: (i, 0, 0))
    tail_shape = jax.ShapeDtypeStruct((b, n_keep, d_attn), F32)
    return pl.pallas_call(
        functools.partial(_attn_kernel, has_hist=has_hist, n_pairs=n_pairs, n_sub=n_sub),
        grid=(b, n_chunks // n_sub),
        in_specs=in_specs,
        out_specs=[pl.BlockSpec((1, n_sub * CHUNK, d_attn), lambda i, c: (i, c, 0)), tail_spec, tail_spec],
        out_shape=[jax.ShapeDtypeStruct((b, s_len, d_attn), BF16), tail_shape, tail_shape],
        scratch_shapes=[pltpu.VMEM((BAND_ROWS + s_len, d_attn), BF16)] * 2,
        compiler_params=_params("parallel", "arbitrary"),
        name="attn_hist" if has_hist else "attn",
    )(*args)


def _bias_pairs(rel_bias):
    n_heads = rel_bias.shape[0]
    n_far = BAND - REL_CLIP
    far = jnp.broadcast_to(rel_bias[:, 2 * REL_CLIP:], (n_heads, n_far))
    near = rel_bias[:, REL_CLIP - (CHUNK - 1):2 * REL_CLIP][:, ::-1]
    ext = jnp.concatenate([far, near], axis=1)
    n_ext = BAND + CHUNK - 1
    flat = jnp.tile(jnp.pad(ext, ((0, 0), (0, 1))), (1, CHUNK))
    bias = flat[:, CHUNK - 1:CHUNK - 1 + CHUNK * n_ext].reshape(n_heads, CHUNK, n_ext)[:, :, :BAND]
    return (bias * LOG2_E).reshape(n_heads // 2, 2 * CHUNK, BAND).astype(F32)


MLSTM_BLK = 128
STATE_ROWS = HEAD_DIM_B + 16


def _sigmoid(x):
    return 0.5 * jnp.tanh(0.5 * x) + 0.5


def _log_sigmoid(x):
    return jnp.minimum(x, 0.0) - jnp.log1p(jnp.exp(-jnp.abs(x)))


def _mm_tn(a, b):
    return lax.dot_general(a, b, (((0,), (0,)), ((), ())), preferred_element_type=F32)


def _mlstm_kernel(q_ref, k_ref, v_ref, og_ref, gates_ref, conv0_ref, c0_ref, n0_ref, m0_ref,
                  wconv_ref, bconv_ref, gbias_ref, gnorm_ref,
                  o_ref, c_out_ref, n_out_ref, m_out_ref,
                  xs, qb_s, k_s, ct_s, m_s, *, n_heads, valid_len):
    blk = MLSTM_BLK
    hd = HEAD_DIM_B
    c = pl.program_id(1)
    last = pl.num_programs(1) - 1
    d_b = n_heads * hd
    pad = SUBLANES
    heads = range(n_heads)

    @pl.when(c == 0)
    def _():
        xs[0:pad, :] = conv0_ref[0]
        for h in heads:
            ct_s[h, 0:hd, :] = c0_ref[0, h].T
            ct_s[h, hd:, :] = jnp.zeros((STATE_ROWS - hd, hd), F32)
            ct_s[h, hd:hd + 1, :] = n0_ref[0, h:h + 1, :]
        m_s[...] = m0_ref[0]

    xs[pad:pad + blk, 0:d_b] = q_ref[0].astype(F32)
    xs[pad:pad + blk, d_b:2 * d_b] = k_ref[0].astype(F32)

    gates = gates_ref[0] + gbias_ref[...]
    ig = gates[:, :LANES]
    lf = _log_sigmoid(gates[:, LANES:])
    if valid_len < blk:
        live = lax.broadcasted_iota(jnp.int32, (blk, LANES), 0) < valid_len
        ig = jnp.where(live, ig, NEG_BIG)
        lf = jnp.where(live, lf, 0.0)
    row_i = lax.broadcasted_iota(jnp.int32, (blk, blk), 0)
    col_i = lax.broadcasted_iota(jnp.int32, (blk, blk), 1)
    f_cum = jnp.dot(jnp.where(row_i >= col_i, 1.0, 0.0).astype(F32), lf,
                    precision=lax.Precision.HIGHEST, preferred_element_type=F32)
    sel_rows = -(-n_heads // SUBLANES) * SUBLANES
    sel = jnp.where(lax.broadcasted_iota(jnp.int32, (sel_rows, LANES), 0)
                    == lax.broadcasted_iota(jnp.int32, (sel_rows, LANES), 1), 1.0, 0.0).astype(F32)
    f_rows = _mm_nt(sel, f_cum, precision=lax.Precision.HIGHEST)
    imf = ig - f_cum
    m_prev = m_s[...]
    f_last = f_cum[blk - 1:blk, :]
    g = f_last + imf
    a_last = f_last + m_prev
    m_new = jnp.maximum(a_last, jnp.max(g, axis=0, keepdims=True))
    w_old = jnp.exp(a_last - m_new)
    wk = jnp.exp(g - m_new)

    for j in range(2 * n_heads):
        cols = slice(j * hd, (j + 1) * hd)
        acc = bconv_ref[:, cols]
        for tap in range(CONV_W):
            acc = acc + xs[pl.ds(pad - (CONV_W - 1) + tap, blk), cols] * wconv_ref[tap:tap + 1, cols]
        y = acc * _sigmoid(acc)
        if j < n_heads:
            qb_s[:, cols] = y.astype(BF16)
        else:
            k_s[:, (j - n_heads) * hd:(j - n_heads + 1) * hd] = y * (hd ** -0.5)

    hcols = [slice(h * hd, (h + 1) * hd) for h in heads]
    st = [_mm_nt(k_s[:, hcols[h]].astype(BF16), qb_s[:, hcols[h]]) for h in heads]
    inter = [_mm_nt(ct_s[h].astype(BF16), qb_s[:, hcols[h]]) for h in heads]

    visible = row_i <= col_i
    f_row = [f_rows[h:h + 1, :] for h in heads]
    a_row = [f_row[h] + m_prev[:, h:h + 1] for h in heads]
    dt = [jnp.where(visible, jnp.broadcast_to(imf[:, h:h + 1], (blk, blk)) + f_row[h], NEG_BIG) for h in heads]
    m_t = [jnp.maximum(a_row[h], jnp.max(dt[h], axis=0, keepdims=True)) for h in heads]
    pt = [st[h] * jnp.exp(dt[h] - m_t[h]) for h in heads]
    w_inter = [jnp.exp(a_row[h] - m_t[h]) for h in heads]
    den = [jnp.sum(pt[h], axis=0, keepdims=True) + w_inter[h] * inter[h][hd:hd + 1, :] for h in heads]
    scale = [1.0 / jnp.maximum(jnp.abs(den[h]), jnp.exp(-m_t[h])) for h in heads]
    kw = [k_s[:, hcols[h]] * jnp.broadcast_to(wk[:, h:h + 1], (blk, hd)) for h in heads]

    num_t = [_mm_tn(v_ref[0, :, hcols[h]], pt[h].astype(BF16)) for h in heads]
    dct = [_mm_tn(v_ref[0, :, hcols[h]], kw[h].astype(BF16)) for h in heads]

    ht = [(num_t[h] + w_inter[h] * inter[h][0:hd, :]) * scale[h] for h in heads]
    ssq = [jnp.sum(ht[h] * ht[h], axis=0, keepdims=True) for h in heads]
    for h in heads:
        hn = (ht[h] * lax.rsqrt(ssq[h] * (1.0 / hd) + 1e-6)).T
        out = hn * gnorm_ref[:, hcols[h]] * _sigmoid(og_ref[0, :, hcols[h]].astype(F32))
        o_ref[0, :, hcols[h]] = out.astype(o_ref.dtype)
        decay = w_old[:, h:h + 1]
        ct_s[h, 0:hd, :] = decay * ct_s[h, 0:hd, :] + dct[h]
        ct_s[h, hd:hd + 1, :] = decay * ct_s[h, hd:hd + 1, :] + jnp.sum(kw[h], axis=0, keepdims=True)
    m_s[...] = m_new
    xs[0:pad, :] = xs[blk:blk + pad, :]

    @pl.when(c == last)
    def _():
        for h in heads:
            c_out_ref[0, h] = ct_s[h, 0:hd, :].T
            n_out_ref[0, h:h + 1, :] = ct_s[h, hd:hd + 1, :]
        m_out_ref[0] = m_s[...]


def _mlstm(z3, gates3, conv0, c0, n0, m0, w_conv, b_conv, gate_bias, g_norm, col_block0, valid_len):
    b, s_len, _ = z3.shape
    n_heads = c0.shape[1]
    d_b = n_heads * HEAD_DIM_B
    blk = MLSTM_BLK
    assert s_len % blk == 0 and n_heads <= LANES and (valid_len == blk or s_len == blk)
    zspec = lambda k: pl.BlockSpec((1, blk, d_b), lambda i, c: (i, c, col_block0 + k))
    full = lambda a: pl.BlockSpec(a.shape, lambda i, c: (0,) * a.ndim)
    per_b = lambda a: pl.BlockSpec((1,) + a.shape[1:], lambda i, c: (i,) + (0,) * (a.ndim - 1))
    return pl.pallas_call(
        functools.partial(_mlstm_kernel, n_heads=n_heads, valid_len=valid_len),
        grid=(b, s_len // blk),
        in_specs=[zspec(0), zspec(1), zspec(2), zspec(3),
                  pl.BlockSpec((1, blk, 2 * LANES), lambda i, c: (i, c, 0)),
                  per_b(conv0), per_b(c0), per_b(n0), per_b(m0),
                  full(w_conv), full(b_conv), full(gate_bias), full(g_norm)],
        out_specs=[pl.BlockSpec((1, blk, d_b), lambda i, c: (i, c, 0)),
                   per_b(c0), per_b(n0), per_b(m0)],
        out_shape=[jax.ShapeDtypeStruct((b, s_len, d_b), BF16),
                   jax.ShapeDtypeStruct(c0.shape, F32),
                   jax.ShapeDtypeStruct(n0.shape, F32),
                   jax.ShapeDtypeStruct(m0.shape, F32)],
        scratch_shapes=[pltpu.VMEM((blk + SUBLANES, 2 * d_b), F32),
                        pltpu.VMEM((blk, d_b), BF16),
                        pltpu.VMEM((blk, d_b), F32),
                        pltpu.VMEM((n_heads, STATE_ROWS, HEAD_DIM_B), F32),
                        pltpu.VMEM((1, LANES), F32)],
        compiler_params=_params("parallel", "arbitrary"),
        name="mlstm",
    )(z3, z3, z3, z3, gates3, conv0, c0, n0, m0, w_conv, b_conv, gate_bias, g_norm)


def _layer_norm(r, g, b):
    mu = jnp.mean(r, axis=-1, keepdims=True)
    rc = r - mu
    var = jnp.mean(rc * rc, axis=-1, keepdims=True)
    return rc * lax.rsqrt(var + 1e-5) * g + b


def _row_splits(n_rows):
    sub = n_rows // LN_SPLITS if n_rows % (LN_SPLITS * 2 * SUBLANES) == 0 else n_rows
    return [slice(r, r + sub) for r in range(0, n_rows, sub)]


def _out_proj_kernel(a_ref, m_ref, x_ref, wa_ref, wm_ref, g_ref, b_ref, x1_ref, x1b_ref, *, alpha):
    for rows in _row_splits(x_ref.shape[0]):
        mix = _mm(a_ref[rows, :], wa_ref[...]) + _mm(m_ref[rows, :], wm_ref[...])
        y = _layer_norm(alpha * x_ref[rows, :] + mix, g_ref[...], b_ref[...])
        x1_ref[rows, :] = y
        x1b_ref[rows, :] = y.astype(BF16)


def _out_proj(attn2d, mlstm2d, x2d, w_out, ln_g, ln_b, alpha, bm):
    t, d = x2d.shape
    da, db = attn2d.shape[1], mlstm2d.shape[1]
    assert da == db
    row = lambda w: pl.BlockSpec((bm, w), lambda i: (i, 0))
    vec = pl.BlockSpec((1, d), lambda i: (0, 0))
    return pl.pallas_call(
        functools.partial(_out_proj_kernel, alpha=alpha),
        grid=(t // bm,),
        in_specs=[row(da), row(db), row(d),
                  pl.BlockSpec((da, d), lambda i: (0, 0)),
                  pl.BlockSpec((db, d), lambda i: (1, 0)),
                  vec, vec],
        out_specs=[row(d), row(d)],
        out_shape=[jax.ShapeDtypeStruct((t, d), F32), jax.ShapeDtypeStruct((t, d), BF16)],
        compiler_params=_params("parallel"),
        name="out_proj_ln",
    )(attn2d, mlstm2d, x2d, w_out, w_out, ln_g, ln_b)


def _ffn_up_kernel(x_ref, wg_ref, wu_ref, h_ref, wgb_ref, wub_ref):
    @pl.when(pl.program_id(1) == 0)
    def _():
        wgb_ref[...] = wg_ref[...].astype(BF16)
        wub_ref[...] = wu_ref[...].astype(BF16)

    x = x_ref[...]
    gate = _mm(x, wgb_ref[...])
    h_ref[...] = (gate * _sigmoid(gate) * _mm(x, wub_ref[...])).astype(h_ref.dtype)


def _ffn_up(x1b, w_gate, w_up, bm, bf):
    t, d = x1b.shape
    f = w_gate.shape[1]
    wspec = pl.BlockSpec((d, bf), lambda j, i: (0, j))
    return pl.pallas_call(
        _ffn_up_kernel,
        grid=(f // bf, t // bm),
        in_specs=[pl.BlockSpec((bm, d), lambda j, i: (i, 0)), wspec, wspec],
        out_specs=pl.BlockSpec((bm, bf), lambda j, i: (i, j)),
        out_shape=jax.ShapeDtypeStruct((t, f), BF16),
        scratch_shapes=[pltpu.VMEM((d, bf), BF16)] * 2,
        compiler_params=_params("parallel", "arbitrary"),
        name="ffn_up",
    )(x1b, w_gate, w_up)


def _ffn_down_kernel(h_ref, wd_ref, x1_ref, g_ref, b_ref, y_ref, *, alpha):
    for rows in _row_splits(x1_ref.shape[0]):
        y_ref[rows, :] = _layer_norm(alpha * x1_ref[rows, :] + _mm(h_ref[rows, :], wd_ref[...]),
                                     g_ref[...], b_ref[...])


def _ffn_down(h, w_down, x1, ln_g, ln_b, alpha, bm):
    t, f = h.shape
    d = w_down.shape[1]
    vec = pl.BlockSpec((1, d), lambda i: (0, 0))
    return pl.pallas_call(
        functools.partial(_ffn_down_kernel, alpha=alpha),
        grid=(t // bm,),
        in_specs=[pl.BlockSpec((bm, f), lambda i: (i, 0)),
                  pl.BlockSpec((f, d), lambda i: (0, 0), pipeline_mode=pl.Buffered(1)),
                  pl.BlockSpec((bm, d), lambda i: (i, 0)),
                  vec, vec],
        out_specs=pl.BlockSpec((bm, d), lambda i: (i, 0)),
        out_shape=jax.ShapeDtypeStruct((t, d), F32),
        compiler_params=_params("parallel"),
        name="ffn_down_ln",
    )(h, w_down, x1, ln_g, ln_b)


def _largest_divisor(n, cap):
    return max(d for d in range(1, min(n, cap) + 1) if n % d == 0)


def _trunk_layer(x, hist_k, hist_v, conv_state, mlstm_state, w, alpha):
    b, s_len, d = x.shape
    t = b * s_len
    c0, n0, m0 = mlstm_state
    n_heads_b = c0.shape[1]
    d_b = n_heads_b * HEAD_DIM_B
    d_attn = w["g_attn"].shape[1]
    n_main = 3 * d_attn + 4 * d_b
    assert d_attn == d_b and w["w_main_t"].shape[0] == n_main

    bm = _largest_divisor(t, 1024)
    x2d = x.reshape(t, d)
    z, gates = _in_proj(x2d, w["w_main_t"], w["w_gate_t"], bm, _largest_divisor(n_main // LANES, 14) * LANES)
    z3 = z.reshape(b, s_len, -1)

    attn, k_tail, v_tail = _attention(z3, hist_k, hist_v, w["bias_pairs"], w["g_attn"])

    conv0 = jnp.pad(conv_state.astype(F32), ((0, 0), (SUBLANES - (CONV_W - 1), 0), (0, 0)))
    m0_row = jnp.pad(m0.astype(F32)[:, None, :], ((0, 0), (0, 0), (0, LANES - n_heads_b)))
    gates3 = gates.reshape(b, s_len, 2 * LANES)
    z3m = z3
    s_pad = -(-s_len // MLSTM_BLK) * MLSTM_BLK
    if s_pad != s_len:
        z3m = jnp.pad(z3, ((0, 0), (0, s_pad - s_len), (0, 0)))
        gates3 = jnp.pad(gates3, ((0, 0), (0, s_pad - s_len), (0, 0)))
    mlstm, c_new, n_new, m_new = _mlstm(
        z3m, gates3, conv0, c0.astype(F32), n0.astype(F32), m0_row,
        w["w_conv"], w["b_conv"], w["gate_bias"], w["g_mlstm"], 3,
        MLSTM_BLK if s_pad == s_len else s_len)
    mlstm = mlstm[:, :s_len]

    bm2 = _largest_divisor(t, 512)
    x1, x1b = _out_proj(attn.reshape(t, d_attn), mlstm.reshape(t, d_b), x2d, w["w_out"],
                        w["ln1_g"], w["ln1_b"], alpha, bm2)
    d_ff = w["w_ffn_gate"].shape[1]
    hmid = _ffn_up(x1b, w["w_ffn_gate"], w["w_ffn_up"], bm, _largest_divisor(d_ff // MXU_COLS, 2) * MXU_COLS)
    y = _ffn_down(hmid, w["w_ffn_down"], x1, w["ln2_g"], w["ln2_b"], alpha, bm2)

    n_heads_a = d_attn // HEAD_DIM_A
    to_heads = lambda a: a.reshape(a.shape[0], a.shape[1], n_heads_a, HEAD_DIM_A)
    qk_raw_tail = z3[:, s_len - (CONV_W - 1):, 3 * d_attn:3 * d_attn + 2 * d_b].astype(F32)
    return (y.reshape(b, s_len, d), to_heads(k_tail), to_heads(v_tail), qk_raw_tail,
            c_new, n_new, m_new[:, 0, :n_heads_b])


def kernel(x_prompt, x_sample, cache_attn_k, cache_attn_v, state_conv, state_mlstm_C, state_mlstm_n, state_mlstm_m, w_in, b_igate, b_fgate, w_conv, b_conv, rel_bias, g_attn_norm, g_mlstm_norm, w_out, ln1_g, ln1_b, w_ffn_gate, w_ffn_up, w_ffn_down, ln2_g, ln2_b):
    depth = w_in.shape[0]
    alpha = (2.0 * depth) ** 0.25
    xp, xs = x_prompt, x_sample
    bp, sp, d_model = xp.shape
    n_heads_b = b_igate.shape[1]
    d_b = n_heads_b * HEAD_DIM_B
    d_attn = g_attn_norm.shape[1]
    n_heads_a = d_attn // HEAD_DIM_A
    n_main = 3 * d_attn + 4 * d_b
    new_p, new_s = [], []
    for l in range(depth):
        q_scale = jnp.where(jnp.arange(n_main) < d_attn, (HEAD_DIM_A ** -0.5) * LOG2_E, 1.0).astype(F32)
        w_in_t = jnp.swapaxes(w_in[l], 0, 1)
        lane_pad = lambda a: jnp.pad(a, ((0, 0), (0, LANES - n_heads_b)))
        row_pad = lambda a: jnp.pad(a, ((0, LANES - n_heads_b), (0, 0)))
        w = dict(
            w_main_t=(w_in_t[:n_main] * q_scale[:, None]).astype(BF16),
            w_gate_t=jnp.concatenate([row_pad(w_in_t[n_main:n_main + n_heads_b]),
                                      row_pad(w_in_t[n_main + n_heads_b:])], axis=0),
            gate_bias=jnp.concatenate([lane_pad(b_igate[l][None, :]), lane_pad(b_fgate[l][None, :])],
                                      axis=1).astype(F32),
            w_conv=w_conv[l], b_conv=b_conv[l][None, :],
            bias_pairs=_bias_pairs(rel_bias[l]),
            g_attn=g_attn_norm[l][None, :], g_mlstm=g_mlstm_norm[l][None, :],
            w_out=w_out[l].astype(BF16), ln1_g=ln1_g[l][None, :], ln1_b=ln1_b[l][None, :],
            w_ffn_gate=w_ffn_gate[l], w_ffn_up=w_ffn_up[l],
            w_ffn_down=w_ffn_down[l].astype(BF16), ln2_g=ln2_g[l][None, :], ln2_b=ln2_b[l][None, :],
        )
        zero_state = (jnp.zeros((bp, n_heads_b, HEAD_DIM_B, HEAD_DIM_B), F32),
                      jnp.zeros((bp, n_heads_b, HEAD_DIM_B), F32),
                      jnp.zeros((bp, n_heads_b), F32))
        zero_conv = jnp.zeros((bp, CONV_W - 1, 2 * d_b), F32)
        xp, *state_p = _trunk_layer(xp, None, None, zero_conv, zero_state, w, alpha)
        new_p.append(tuple(state_p))
        bs = xs.shape[0]
        ck = cache_attn_k[l].astype(F32).reshape(bs, -1, d_attn)
        cv = cache_attn_v[l].astype(F32).reshape(bs, -1, d_attn)
        assert ck.shape[1] == BAND_ROWS
        xs, *state_s = _trunk_layer(xs, ck, cv, state_conv[l],
                                    (state_mlstm_C[l], state_mlstm_n[l], state_mlstm_m[l]), w, alpha)
        new_s.append(tuple(state_s))
    k_p, v_p, conv_p, C_p, n_p, m_p = [jnp.stack(t) for t in zip(*new_p)]
    k_s, v_s, conv_s, C_s, n_s, m_s = [jnp.stack(t) for t in zip(*new_s)]
    return (xp, xs, k_p, v_p, conv_p, C_p, n_p, m_p, k_s, v_s, conv_s, C_s, n_s, m_s)
```

```python
import functools

import jax
import jax.numpy as jnp
from jax import lax
from jax.experimental import pallas as pl
from jax.experimental.pallas import tpu as pltpu

CHUNK = 64
BAND_ROWS = 8 * CHUNK
BAND = BAND_ROWS + CHUNK
REL_CLIP = 256
HEAD_DIM_A = 64
HEAD_DIM_B = 128
CONV_W = 4
LANES = 128
SUBLANES = 8
MXU_COLS = 256
NEG_BIG = -1e30
LOG2_E = 1.4426950408889634
LN_SPLITS = 4
ATTN_CHUNKS_PER_STEP = 4
VMEM_LIMIT_BYTES = 56 * 1024 * 1024

F32 = jnp.float32
BF16 = jnp.bfloat16


def _params(*semantics, flags=None):
    return pltpu.CompilerParams(dimension_semantics=semantics, vmem_limit_bytes=VMEM_LIMIT_BYTES,
                                flags=flags)


def _mm(a, b):
    return jnp.dot(a, b, preferred_element_type=F32)


def _mm_nt(a, b, precision=None):
    return lax.dot_general(a, b, (((1,), (1,)), ((), ())), precision=precision,
                           preferred_element_type=F32)


def _in_proj_kernel(x_ref, wt_ref, wgt_ref, z_ref, g_ref, xb_ref):
    @pl.when(pl.program_id(1) == 0)
    def _():
        xb = x_ref[...].astype(BF16)
        xb_ref[...] = xb
        g_ref[...] = _mm_nt(xb, wgt_ref[...].astype(BF16))

    z_ref[...] = _mm_nt(xb_ref[...], wt_ref[...]).astype(z_ref.dtype)


def _in_proj(x2d, w_main_t, w_gate_t, bm, bn):
    t, d = x2d.shape
    n = w_main_t.shape[0]
    ng = w_gate_t.shape[0]
    return pl.pallas_call(
        _in_proj_kernel,
        grid=(t // bm, n // bn),
        in_specs=[pl.BlockSpec((bm, d), lambda i, j: (i, 0)),
                  pl.BlockSpec((bn, d), lambda i, j: (j, 0)),
                  pl.BlockSpec((ng, d), lambda i, j: (0, 0))],
        out_specs=[pl.BlockSpec((bm, bn), lambda i, j: (i, j)),
                   pl.BlockSpec((bm, ng), lambda i, j: (i, 0))],
        out_shape=[jax.ShapeDtypeStruct((t, n), BF16),
                   jax.ShapeDtypeStruct((t, ng), F32)],
        scratch_shapes=[pltpu.VMEM((bm, d), BF16)],
        compiler_params=_params("parallel", "arbitrary"),
        name="in_proj",
    )(x2d, w_main_t, w_gate_t)


def _attn_kernel(*refs, has_hist, n_pairs, n_sub):
    if has_hist:
        q_ref, k_ref, v_ref, kh_ref, vh_ref, bias_ref, g_ref, o_ref, kt_ref, vt_ref, kpad, vpad = refs
    else:
        q_ref, k_ref, v_ref, bias_ref, g_ref, o_ref, kt_ref, vt_ref, kpad, vpad = refs
    step = pl.program_id(1)
    s_len = k_ref.shape[1]
    n_keep = kt_ref.shape[1]
    n_new = min(n_keep, s_len)

    @pl.when(step == 0)
    def _():
        if has_hist:
            kpad[0:BAND_ROWS, :] = kh_ref[0].astype(BF16)
            vpad[0:BAND_ROWS, :] = vh_ref[0].astype(BF16)
            if n_new < n_keep:
                kt_ref[0, 0:n_keep - n_new, :] = kh_ref[0, BAND_ROWS - (n_keep - n_new):, :]
                vt_ref[0, 0:n_keep - n_new, :] = vh_ref[0, BAND_ROWS - (n_keep - n_new):, :]
        else:
            kpad[0:BAND_ROWS, :] = jnp.zeros((BAND_ROWS, kpad.shape[1]), BF16)
            vpad[0:BAND_ROWS, :] = jnp.zeros((BAND_ROWS, vpad.shape[1]), BF16)
        kpad[BAND_ROWS:BAND_ROWS + s_len, :] = k_ref[0]
        vpad[BAND_ROWS:BAND_ROWS + s_len, :] = v_ref[0]
        kt_ref[0, n_keep - n_new:, :] = k_ref[0, s_len - n_new:, :].astype(F32)
        vt_ref[0, n_keep - n_new:, :] = v_ref[0, s_len - n_new:, :].astype(F32)

    def chunks(masked):
        for u in range(n_sub):
            _attn_chunk(q_ref, kpad, vpad, bias_ref, g_ref, o_ref, step * n_sub + u, u, n_pairs, masked)

    if has_hist:
        chunks(masked=False)
    else:
        first_steps = BAND_ROWS // CHUNK // n_sub
        pl.when(step < first_steps)(functools.partial(chunks, masked=True))
        pl.when(step >= first_steps)(functools.partial(chunks, masked=False))


def _attn_chunk(q_ref, kpad, vpad, bias_ref, g_ref, o_ref, c, u, n_pairs, masked):
    start = pl.multiple_of(c * CHUNK, CHUNK)
    rows = slice(u * CHUNK, (u + 1) * CHUNK)
    lane = lax.broadcasted_iota(jnp.int32, (CHUNK, LANES), 1)
    row2 = lax.broadcasted_iota(jnp.int32, (2 * CHUNK, LANES), 0)
    lane2 = lax.broadcasted_iota(jnp.int32, (2 * CHUNK, LANES), 1)
    own_head = (row2 >= CHUNK) == (lane2 >= HEAD_DIM_A)
    if masked:
        col = lax.broadcasted_iota(jnp.int32, (2 * CHUNK, BAND), 1)
        valid = col >= BAND_ROWS - c * CHUNK

    def scores(p):
        cols = slice(p * LANES, (p + 1) * LANES)
        qp = q_ref[0, rows, cols]
        zero = jnp.zeros_like(qp)
        q2 = jnp.concatenate([jnp.where(lane < HEAD_DIM_A, qp, zero),
                              jnp.where(lane >= HEAD_DIM_A, qp, zero)], axis=0)
        return _mm_nt(q2, kpad[pl.ds(start, BAND), cols])

    pairs = range(n_pairs)
    s = [scores(p) + bias_ref[p] for p in pairs]
    if masked:
        s = [jnp.where(valid, s[p], NEG_BIG) for p in pairs]
    m = [jnp.max(s[p], axis=-1, keepdims=True) for p in pairs]
    e = [jnp.exp2((s[p] - m[p]).astype(BF16)) for p in pairs]
    ones = jnp.ones((BAND, LANES), BF16)
    o = [_mm(e[p], jnp.concatenate([vpad[pl.ds(start, BAND), p * LANES:(p + 1) * LANES], ones], axis=1))
         for p in pairs]
    o = [jnp.where(own_head, o[p][:, :LANES] / o[p][:, LANES:], 0.0) for p in pairs]
    ssq = [jnp.sum(o[p] * o[p], axis=-1, keepdims=True) for p in pairs]
    for p in pairs:
        cols = slice(p * LANES, (p + 1) * LANES)
        on = o[p] * lax.rsqrt(ssq[p] * (1.0 / HEAD_DIM_A) + 1e-6)
        o_ref[0, rows, cols] = ((on[:CHUNK] + on[CHUNK:]) * g_ref[:, cols]).astype(o_ref.dtype)


def _attention(z3, hist_k, hist_v, bias_pairs, g_attn):
    b, s_len, _ = z3.shape
    d_attn = g_attn.shape[1]
    n_pairs = d_attn // LANES
    has_hist = hist_k is not None
    n_keep = BAND_ROWS if has_hist else min(BAND_ROWS, s_len)
    n_chunks = s_len // CHUNK
    n_sub = ATTN_CHUNKS_PER_STEP if n_chunks % ATTN_CHUNKS_PER_STEP == 0 else 1
    assert (BAND_ROWS // CHUNK) % n_sub == 0
    in_specs = [pl.BlockSpec((1, n_sub * CHUNK, d_attn), lambda i, c: (i, c, 0)),
                pl.BlockSpec((1, s_len, d_attn), lambda i, c: (i, 0, 1)),
                pl.BlockSpec((1, s_len, d_attn), lambda i, c: (i, 0, 2))]
    args = [z3, z3, z3]
    if has_hist:
        in_specs += [pl.BlockSpec((1, BAND_ROWS, d_attn), lambda i, c: (i, 0, 0))] * 2
        args += [hist_k, hist_v]
    in_specs += [pl.BlockSpec(bias_pairs.shape, lambda i, c: (0, 0, 0)),
                 pl.BlockSpec((1, d_attn), lambda i, c: (0, 0))]
    args += [bias_pairs, g_attn]
    tail_spec = pl.BlockSpec((1, n_keep, d_attn), lambda i, c: (i, 0, 0))
    tail_shape = jax.ShapeDtypeStruct((b, n_keep, d_attn), F32)
    return pl.pallas_call(
        functools.partial(_attn_kernel, has_hist=has_hist, n_pairs=n_pairs, n_sub=n_sub),
        grid=(b, n_chunks // n_sub),
        in_specs=in_specs,
        out_specs=[pl.BlockSpec((1, n_sub * CHUNK, d_attn), lambda i, c: (i, c, 0)), tail_spec, tail_spec],
        out_shape=[jax.ShapeDtypeStruct((b, s_len, d_attn), BF16), tail_shape, tail_shape],
        scratch_shapes=[pltpu.VMEM((BAND_ROWS + s_len, d_attn), BF16)] * 2,
        compiler_params=_params("parallel", "arbitrary"),
        name="attn_hist" if has_hist else "attn",
    )(*args)


def _bias_pairs(rel_bias):
    n_heads = rel_bias.shape[0]
    n_far = BAND - REL_CLIP
    far = jnp.broadcast_to(rel_bias[:, 2 * REL_CLIP:], (n_heads, n_far))
    near = rel_bias[:, REL_CLIP - (CHUNK - 1):2 * REL_CLIP][:, ::-1]
    ext = jnp.concatenate([far, near], axis=1)
    n_ext = BAND + CHUNK - 1
    flat = jnp.tile(jnp.pad(ext, ((0, 0), (0, 1))), (1, CHUNK))
    bias = flat[:, CHUNK - 1:CHUNK - 1 + CHUNK * n_ext].reshape(n_heads, CHUNK, n_ext)[:, :, :BAND]
    return (bias * LOG2_E).reshape(n_heads // 2, 2 * CHUNK, BAND).astype(F32)


MLSTM_BLK = 128
STATE_ROWS = HEAD_DIM_B + 16


def _sigmoid(x):
    return 0.5 * jnp.tanh(0.5 * x) + 0.5


def _log_sigmoid(x):
    return jnp.minimum(x, 0.0) - jnp.log1p(jnp.exp(-jnp.abs(x)))


def _mm_tn(a, b):
    return lax.dot_general(a, b, (((0,), (0,)), ((), ())), preferred_element_type=F32)


def _mlstm_kernel(q_ref, k_ref, v_ref, og_ref, gates_ref, conv0_ref, c0_ref, n0_ref, m0_ref,
                  wconv_ref, bconv_ref, gbias_ref, gnorm_ref,
                  o_ref, c_out_ref, n_out_ref, m_out_ref,
                  xs, qb_s, k_s, ct_s, m_s, *, n_heads, valid_len):
    blk = MLSTM_BLK
    hd = HEAD_DIM_B
    c = pl.program_id(1)
    last = pl.num_programs(1) - 1
    d_b = n_heads * hd
    pad = SUBLANES
    heads = range(n_heads)

    @pl.when(c == 0)
    def _():
        xs[0:pad, :] = conv0_ref[0]
        for h in heads:
            ct_s[h, 0:hd, :] = c0_ref[0, h].T
            ct_s[h, hd:, :] = jnp.zeros((STATE_ROWS - hd, hd), F32)
            ct_s[h, hd:hd + 1, :] = n0_ref[0, h:h + 1, :]
        m_s[...] = m0_ref[0]

    def rows_to_blk(a):
        if valid_len == blk:
            return a
        return jnp.concatenate([a, jnp.zeros((blk - valid_len,) + a.shape[1:], a.dtype)], axis=0)

    xs[pad:pad + blk, 0:d_b] = rows_to_blk(q_ref[0].astype(F32))
    xs[pad:pad + blk, d_b:2 * d_b] = rows_to_blk(k_ref[0].astype(F32))

    gates = rows_to_blk(gates_ref[0]) + gbias_ref[...]
    ig = gates[:, :LANES]
    lf = _log_sigmoid(gates[:, LANES:])
    if valid_len < blk:
        live = lax.broadcasted_iota(jnp.int32, (blk, LANES), 0) < valid_len
        ig = jnp.where(live, ig, NEG_BIG)
        lf = jnp.where(live, lf, 0.0)
    row_i = lax.broadcasted_iota(jnp.int32, (blk, blk), 0)
    col_i = lax.broadcasted_iota(jnp.int32, (blk, blk), 1)
    f_cum = jnp.dot(jnp.where(row_i >= col_i, 1.0, 0.0).astype(F32), lf,
                    precision=lax.Precision.HIGHEST, preferred_element_type=F32)
    sel_rows = -(-n_heads // SUBLANES) * SUBLANES
    sel = jnp.where(lax.broadcasted_iota(jnp.int32, (sel_rows, LANES), 0)
                    == lax.broadcasted_iota(jnp.int32, (sel_rows, LANES), 1), 1.0, 0.0).astype(F32)
    f_rows = _mm_nt(sel, f_cum, precision=lax.Precision.HIGHEST)
    imf = ig - f_cum
    m_prev = m_s[...]
    f_last = f_cum[blk - 1:blk, :]
    g = f_last + imf
    a_last = f_last + m_prev
    m_new = jnp.maximum(a_last, jnp.max(g, axis=0, keepdims=True))
    w_old = jnp.exp(a_last - m_new)
    wk = jnp.exp(g - m_new)

    for j in range(2 * n_heads):
        cols = slice(j * hd, (j + 1) * hd)
        acc = bconv_ref[:, cols]
        for tap in range(CONV_W):
            acc = acc + xs[pl.ds(pad - (CONV_W - 1) + tap, blk), cols] * wconv_ref[tap:tap + 1, cols]
        y = acc * _sigmoid(acc)
        if j < n_heads:
            qb_s[:, cols] = y.astype(BF16)
        else:
            k_s[:, (j - n_heads) * hd:(j - n_heads + 1) * hd] = y * (hd ** -0.5)

    hcols = [slice(h * hd, (h + 1) * hd) for h in heads]
    st = [_mm_nt(k_s[:, hcols[h]].astype(BF16), qb_s[:, hcols[h]]) for h in heads]
    inter = [_mm_nt(ct_s[h].astype(BF16), qb_s[:, hcols[h]]) for h in heads]

    visible = row_i <= col_i
    f_row = [f_rows[h:h + 1, :] for h in heads]
    a_row = [f_row[h] + m_prev[:, h:h + 1] for h in heads]
    dt = [jnp.where(visible, jnp.broadcast_to(imf[:, h:h + 1], (blk, blk)) + f_row[h], NEG_BIG) for h in heads]
    m_t = [jnp.maximum(a_row[h], jnp.max(dt[h], axis=0, keepdims=True)) for h in heads]
    pt = [st[h] * jnp.exp(dt[h] - m_t[h]) for h in heads]
    w_inter = [jnp.exp(a_row[h] - m_t[h]) for h in heads]
    den = [jnp.sum(pt[h], axis=0, keepdims=True) + w_inter[h] * inter[h][hd:hd + 1, :] for h in heads]
    scale = [1.0 / jnp.maximum(jnp.abs(den[h]), jnp.exp(-m_t[h])) for h in heads]
    kw = [k_s[:, hcols[h]] * jnp.broadcast_to(wk[:, h:h + 1], (blk, hd)) for h in heads]

    v = [rows_to_blk(v_ref[0, :, hcols[h]]) for h in heads]
    num_t = [_mm_tn(v[h], pt[h].astype(BF16)) for h in heads]
    dct = [_mm_tn(v[h], kw[h].astype(BF16)) for h in heads]

    ht = [(num_t[h] + w_inter[h] * inter[h][0:hd, :]) * scale[h] for h in heads]
    ssq = [jnp.sum(ht[h] * ht[h], axis=0, keepdims=True) for h in heads]
    for h in heads:
        hn = (ht[h] * lax.rsqrt(ssq[h] * (1.0 / hd) + 1e-6)).T[0:valid_len]
        out = hn * gnorm_ref[:, hcols[h]] * _sigmoid(og_ref[0, :, hcols[h]].astype(F32))
        o_ref[0, :, hcols[h]] = out.astype(o_ref.dtype)
        decay = w_old[:, h:h + 1]
        ct_s[h, 0:hd, :] = decay * ct_s[h, 0:hd, :] + dct[h]
        ct_s[h, hd:hd + 1, :] = decay * ct_s[h, hd:hd + 1, :] + jnp.sum(kw[h], axis=0, keepdims=True)
    m_s[...] = m_new
    xs[0:pad, :] = xs[blk:blk + pad, :]

    @pl.when(c == last)
    def _():
        for h in heads:
            c_out_ref[0, h] = ct_s[h, 0:hd, :].T
            n_out_ref[0, h:h + 1, :] = ct_s[h, hd:hd + 1, :]
        m_out_ref[0] = m_s[...]


def _mlstm(z3, gates3, conv0, c0, n0, m0, w_conv, b_conv, gate_bias, g_norm, col_block0):
    b, s_len, _ = z3.shape
    n_heads = c0.shape[1]
    d_b = n_heads * HEAD_DIM_B
    rows = min(s_len, MLSTM_BLK)
    assert s_len % rows == 0 and rows % (2 * SUBLANES) == 0 and n_heads <= LANES
    zspec = lambda k: pl.BlockSpec((1, rows, d_b), lambda i, c: (i, c, col_block0 + k))
    full = lambda a: pl.BlockSpec(a.shape, lambda i, c: (0,) * a.ndim)
    per_b = lambda a: pl.BlockSpec((1,) + a.shape[1:], lambda i, c: (i,) + (0,) * (a.ndim - 1))
    blk = MLSTM_BLK
    return pl.pallas_call(
        functools.partial(_mlstm_kernel, n_heads=n_heads, valid_len=rows),
        grid=(b, s_len // rows),
        in_specs=[zspec(0), zspec(1), zspec(2), zspec(3),
                  pl.BlockSpec((1, rows, 2 * LANES), lambda i, c: (i, c, 0)),
                  per_b(conv0), per_b(c0), per_b(n0), per_b(m0),
                  full(w_conv), full(b_conv), full(gate_bias), full(g_norm)],
        out_specs=[pl.BlockSpec((1, rows, d_b), lambda i, c: (i, c, 0)),
                   per_b(c0), per_b(n0), per_b(m0)],
        out_shape=[jax.ShapeDtypeStruct((b, s_len, d_b), BF16),
                   jax.ShapeDtypeStruct(c0.shape, F32),
                   jax.ShapeDtypeStruct(n0.shape, F32),
                   jax.ShapeDtypeStruct(m0.shape, F32)],
        scratch_shapes=[pltpu.VMEM((blk + SUBLANES, 2 * d_b), F32),
                        pltpu.VMEM((blk, d_b), BF16),
                        pltpu.VMEM((blk, d_b), F32),
                        pltpu.VMEM((n_heads, STATE_ROWS, HEAD_DIM_B), F32),
                        pltpu.VMEM((1, LANES), F32)],
        compiler_params=_params("parallel", "arbitrary"),
        name="mlstm",
    )(z3, z3, z3, z3, gates3, conv0, c0, n0, m0, w_conv, b_conv, gate_bias, g_norm)


def _layer_norm(r, g, b):
    mu = jnp.mean(r, axis=-1, keepdims=True)
    rc = r - mu
    var = jnp.mean(rc * rc, axis=-1, keepdims=True)
    return rc * lax.rsqrt(var + 1e-5) * g + b


def _row_splits(n_rows):
    sub = n_rows // LN_SPLITS if n_rows % (LN_SPLITS * 2 * SUBLANES) == 0 else n_rows
    return [slice(r, r + sub) for r in range(0, n_rows, sub)]


def _out_proj_kernel(a_ref, m_ref, x_ref, wa_ref, wm_ref, g_ref, b_ref, x1_ref, x1b_ref, *, alpha):
    for rows in _row_splits(x_ref.shape[0]):
        mix = _mm(a_ref[rows, :], wa_ref[...]) + _mm(m_ref[rows, :], wm_ref[...])
        y = _layer_norm(alpha * x_ref[rows, :] + mix, g_ref[...], b_ref[...])
        x1_ref[rows, :] = y
        x1b_ref[rows, :] = y.astype(BF16)


def _out_proj(attn2d, mlstm2d, x2d, w_out, ln_g, ln_b, alpha, bm):
    t, d = x2d.shape
    da, db = attn2d.shape[1], mlstm2d.shape[1]
    assert da == db
    row = lambda w: pl.BlockSpec((bm, w), lambda i: (i, 0))
    vec = pl.BlockSpec((1, d), lambda i: (0, 0))
    return pl.pallas_call(
        functools.partial(_out_proj_kernel, alpha=alpha),
        grid=(t // bm,),
        in_specs=[row(da), row(db), row(d),
                  pl.BlockSpec((da, d), lambda i: (0, 0)),
                  pl.BlockSpec((db, d), lambda i: (1, 0)),
                  vec, vec],
        out_specs=[row(d), row(d)],
        out_shape=[jax.ShapeDtypeStruct((t, d), F32), jax.ShapeDtypeStruct((t, d), BF16)],
        compiler_params=_params("parallel"),
        name="out_proj_ln",
    )(attn2d, mlstm2d, x2d, w_out, w_out, ln_g, ln_b)


def _ffn_up_kernel(x_ref, wg_ref, wu_ref, h_ref, wgb_ref, wub_ref):
    if wgb_ref is not None:
        @pl.when(pl.program_id(1) == 0)
        def _():
            wgb_ref[...] = wg_ref[...].astype(BF16)
            wub_ref[...] = wu_ref[...].astype(BF16)
        wg_ref, wu_ref = wgb_ref, wub_ref

    x = x_ref[...]
    gate = _mm(x, wg_ref[...])
    h_ref[...] = (gate * _sigmoid(gate) * _mm(x, wu_ref[...])).astype(h_ref.dtype)


def _ffn_up(x1b, w_gate, w_up, bm, bf):
    t, d = x1b.shape
    f = w_gate.shape[1]
    wspec = pl.BlockSpec((d, bf), lambda j, i: (0, j))
    hspec = pl.BlockSpec((bm, bf), lambda j, i: (i, j))
    hshape = jax.ShapeDtypeStruct((t, f), BF16)
    cast = w_gate.dtype != BF16
    kernel_fn = _ffn_up_kernel if cast else functools.partial(_ffn_up_kernel, wgb_ref=None, wub_ref=None)
    return pl.pallas_call(
        kernel_fn,
        grid=(f // bf, t // bm),
        in_specs=[pl.BlockSpec((bm, d), lambda j, i: (i, 0)), wspec, wspec],
        out_specs=[hspec, wspec, wspec] if cast else [hspec],
        out_shape=[hshape] + [jax.ShapeDtypeStruct((d, f), BF16)] * 2 if cast else [hshape],
        compiler_params=_params("parallel", "arbitrary"),
        name="ffn_up",
    )(x1b, w_gate, w_up)


def _ffn_down_kernel(h_ref, wd_ref, x1_ref, g_ref, b_ref, y_ref, *, alpha):
    for rows in _row_splits(x1_ref.shape[0]):
        y_ref[rows, :] = _layer_norm(alpha * x1_ref[rows, :] + _mm(h_ref[rows, :], wd_ref[...]),
                                     g_ref[...], b_ref[...])


def _ffn_down(h, w_down, x1, ln_g, ln_b, alpha, bm):
    t, f = h.shape
    d = w_down.shape[1]
    vec = pl.BlockSpec((1, d), lambda i: (0, 0))
    return pl.pallas_call(
        functools.partial(_ffn_down_kernel, alpha=alpha),
        grid=(t // bm,),
        in_specs=[pl.BlockSpec((bm, f), lambda i: (i, 0)),
                  pl.BlockSpec((f, d), lambda i: (0, 0), pipeline_mode=pl.Buffered(1)),
                  pl.BlockSpec((bm, d), lambda i: (i, 0)),
                  vec, vec],
        out_specs=pl.BlockSpec((bm, d), lambda i: (i, 0)),
        out_shape=jax.ShapeDtypeStruct((t, d), F32),
        compiler_params=_params("parallel"),
        name="ffn_down_ln",
    )(h, w_down, x1, ln_g, ln_b)


def _largest_divisor(n, cap):
    return max(d for d in range(1, min(n, cap) + 1) if n % d == 0)


def _trunk_layer(x, hist_k, hist_v, conv_state, mlstm_state, w, alpha):
    b, s_len, d = x.shape
    t = b * s_len
    c0, n0, m0 = mlstm_state
    n_heads_b = c0.shape[1]
    d_b = n_heads_b * HEAD_DIM_B
    d_attn = w["g_attn"].shape[1]
    n_main = 3 * d_attn + 4 * d_b
    assert d_attn == d_b and w["w_main_t"].shape[0] == n_main

    bm = _largest_divisor(t, 1024)
    x2d = x.reshape(t, d)
    z, gates = _in_proj(x2d, w["w_main_t"], w["w_gate_t"], bm, _largest_divisor(n_main // LANES, 14) * LANES)
    z3 = z.reshape(b, s_len, -1)

    attn, k_tail, v_tail = _attention(z3, hist_k, hist_v, w["bias_pairs"], w["g_attn"])

    conv0 = jnp.pad(conv_state.astype(F32), ((0, 0), (SUBLANES - (CONV_W - 1), 0), (0, 0)))
    m0_row = jnp.pad(m0.astype(F32)[:, None, :], ((0, 0), (0, 0), (0, LANES - n_heads_b)))
    mlstm, c_new, n_new, m_new = _mlstm(
        z3, gates.reshape(b, s_len, 2 * LANES), conv0, c0.astype(F32), n0.astype(F32), m0_row,
        w["w_conv"], w["b_conv"], w["gate_bias"], w["g_mlstm"], 3)

    bm2 = _largest_divisor(t, 512)
    x1, x1b = _out_proj(attn.reshape(t, d_attn), mlstm.reshape(t, d_b), x2d, w["w_out"],
                        w["ln1_g"], w["ln1_b"], alpha, bm2)
    d_ff = w["w_ffn_gate"].shape[1]
    hmid, *w_bf16 = _ffn_up(x1b, w["w_ffn_gate"], w["w_ffn_up"], bm,
                            _largest_divisor(d_ff // MXU_COLS, 2) * MXU_COLS)
    if w_bf16:
        w["w_ffn_gate"], w["w_ffn_up"] = w_bf16
    y = _ffn_down(hmid, w["w_ffn_down"], x1, w["ln2_g"], w["ln2_b"], alpha, bm2)

    n_heads_a = d_attn // HEAD_DIM_A
    to_heads = lambda a: a.reshape(a.shape[0], a.shape[1], n_heads_a, HEAD_DIM_A)
    qk_raw_tail = z3[:, s_len - (CONV_W - 1):, 3 * d_attn:3 * d_attn + 2 * d_b].astype(F32)
    return (y.reshape(b, s_len, d), to_heads(k_tail), to_heads(v_tail), qk_raw_tail,
            c_new, n_new, m_new[:, 0, :n_heads_b])


def kernel(x_prompt, x_sample, cache_attn_k, cache_attn_v, state_conv, state_mlstm_C, state_mlstm_n, state_mlstm_m, w_in, b_igate, b_fgate, w_conv, b_conv, rel_bias, g_attn_norm, g_mlstm_norm, w_out, ln1_g, ln1_b, w_ffn_gate, w_ffn_up, w_ffn_down, ln2_g, ln2_b):
    depth = w_in.shape[0]
    alpha = (2.0 * depth) ** 0.25
    xp, xs = x_prompt, x_sample
    bp, sp, d_model = xp.shape
    n_heads_b = b_igate.shape[1]
    d_b = n_heads_b * HEAD_DIM_B
    d_attn = g_attn_norm.shape[1]
    n_heads_a = d_attn // HEAD_DIM_A
    n_main = 3 * d_attn + 4 * d_b
    new_p, new_s = [], []
    for l in range(depth):
        q_scale = jnp.where(jnp.arange(n_main) < d_attn, (HEAD_DIM_A ** -0.5) * LOG2_E, 1.0).astype(F32)
        w_in_t = jnp.swapaxes(w_in[l], 0, 1)
        lane_pad = lambda a: jnp.pad(a, ((0, 0), (0, LANES - n_heads_b)))
        row_pad = lambda a: jnp.pad(a, ((0, LANES - n_heads_b), (0, 0)))
        w = dict(
            w_main_t=(w_in_t[:n_main] * q_scale[:, None]).astype(BF16),
            w_gate_t=jnp.concatenate([row_pad(w_in_t[n_main:n_main + n_heads_b]),
                                      row_pad(w_in_t[n_main + n_heads_b:])], axis=0),
            gate_bias=jnp.concatenate([lane_pad(b_igate[l][None, :]), lane_pad(b_fgate[l][None, :])],
                                      axis=1).astype(F32),
            w_conv=w_conv[l], b_conv=b_conv[l][None, :],
            bias_pairs=_bias_pairs(rel_bias[l]),
            g_attn=g_attn_norm[l][None, :], g_mlstm=g_mlstm_norm[l][None, :],
            w_out=w_out[l].astype(BF16), ln1_g=ln1_g[l][None, :], ln1_b=ln1_b[l][None, :],
            w_ffn_gate=w_ffn_gate[l], w_ffn_up=w_ffn_up[l],
            w_ffn_down=w_ffn_down[l].astype(BF16), ln2_g=ln2_g[l][None, :], ln2_b=ln2_b[l][None, :],
        )
        zero_state = (jnp.zeros((bp, n_heads_b, HEAD_DIM_B, HEAD_DIM_B), F32),
                      jnp.zeros((bp, n_heads_b, HEAD_DIM_B), F32),
                      jnp.zeros((bp, n_heads_b), F32))
        zero_conv = jnp.zeros((bp, CONV_W - 1, 2 * d_b), F32)
        xp, *state_p = _trunk_layer(xp, None, None, zero_conv, zero_state, w, alpha)
        new_p.append(tuple(state_p))
        bs = xs.shape[0]
        ck = cache_attn_k[l].astype(F32).reshape(bs, -1, d_attn)
        cv = cache_attn_v[l].astype(F32).reshape(bs, -1, d_attn)
        assert ck.shape[1] == BAND_ROWS
        xs, *state_s = _trunk_layer(xs, ck, cv, state_conv[l],
                                    (state_mlstm_C[l], state_mlstm_n[l], state_mlstm_m[l]), w, alpha)
        new_s.append(tuple(state_s))
    k_p, v_p, conv_p, C_p, n_p, m_p = [jnp.stack(t) for t in zip(*new_p)]
    k_s, v_s, conv_s, C_s, n_s, m_s = [jnp.stack(t) for t in zip(*new_s)]
    return (xp, xs, k_p, v_p, conv_p, C_p, n_p, m_p, k_s, v_s, conv_s, C_s, n_s, m_s)
```

```python
import functools

import jax
import jax.numpy as jnp
from jax import lax
from jax.experimental import pallas as pl
from jax.experimental.pallas import tpu as pltpu

CHUNK = 64
BAND_ROWS = 8 * CHUNK
BAND = BAND_ROWS + CHUNK
REL_CLIP = 256
HEAD_DIM_A = 64
HEAD_DIM_B = 128
CONV_W = 4
LANES = 128
SUBLANES = 8
MXU_COLS = 256
NEG_BIG = -1e30
LOG2_E = 1.4426950408889634
LN_SPLITS = 4
ATTN_CHUNKS_PER_STEP = 8
VMEM_LIMIT_BYTES = 56 * 1024 * 1024

F32 = jnp.float32
BF16 = jnp.bfloat16


def _params(*semantics, flags=None):
    return pltpu.CompilerParams(dimension_semantics=semantics, vmem_limit_bytes=VMEM_LIMIT_BYTES,
                                flags=flags)


def _mm(a, b):
    return jnp.dot(a, b, preferred_element_type=F32)


def _mm_nt(a, b, precision=None):
    return lax.dot_general(a, b, (((1,), (1,)), ((), ())), precision=precision,
                           preferred_element_type=F32)


def _in_proj_kernel(x_ref, wt_ref, wgt_ref, z_ref, g_ref, xb_ref):
    @pl.when(pl.program_id(1) == 0)
    def _():
        xb = x_ref[...].astype(BF16)
        xb_ref[...] = xb
        g_ref[...] = _mm_nt(xb, wgt_ref[...].astype(BF16))

    z_ref[...] = _mm_nt(xb_ref[...], wt_ref[...]).astype(z_ref.dtype)


def _in_proj(x2d, w_main_t, w_gate_t, bm, bn):
    t, d = x2d.shape
    n = w_main_t.shape[0]
    ng = w_gate_t.shape[0]
    return pl.pallas_call(
        _in_proj_kernel,
        grid=(t // bm, n // bn),
        in_specs=[pl.BlockSpec((bm, d), lambda i, j: (i, 0)),
                  pl.BlockSpec((bn, d), lambda i, j: (j, 0)),
                  pl.BlockSpec((ng, d), lambda i, j: (0, 0))],
        out_specs=[pl.BlockSpec((bm, bn), lambda i, j: (i, j)),
                   pl.BlockSpec((bm, ng), lambda i, j: (i, 0))],
        out_shape=[jax.ShapeDtypeStruct((t, n), BF16),
                   jax.ShapeDtypeStruct((t, ng), F32)],
        scratch_shapes=[pltpu.VMEM((bm, d), BF16)],
        compiler_params=_params("parallel", "arbitrary"),
        name="in_proj",
    )(x2d, w_main_t, w_gate_t)


def _attn_kernel(*refs, has_hist, n_pairs, n_sub):
    if has_hist:
        q_ref, k_ref, v_ref, kh_ref, vh_ref, bias_ref, g_ref, o_ref, kt_ref, vt_ref, kpad, vpad = refs
    else:
        q_ref, k_ref, v_ref, bias_ref, g_ref, o_ref, kt_ref, vt_ref, kpad, vpad = refs
    step = pl.program_id(1)
    s_len = k_ref.shape[1]
    n_keep = kt_ref.shape[1]
    n_new = min(n_keep, s_len)

    @pl.when(step == 0)
    def _():
        if has_hist:
            kpad[0:BAND_ROWS, :] = kh_ref[0].astype(BF16)
            vpad[0:BAND_ROWS, :] = vh_ref[0].astype(BF16)
            if n_new < n_keep:
                kt_ref[0, 0:n_keep - n_new, :] = kh_ref[0, BAND_ROWS - (n_keep - n_new):, :]
                vt_ref[0, 0:n_keep - n_new, :] = vh_ref[0, BAND_ROWS - (n_keep - n_new):, :]
        else:
            kpad[0:BAND_ROWS, :] = jnp.zeros((BAND_ROWS, kpad.shape[1]), BF16)
            vpad[0:BAND_ROWS, :] = jnp.zeros((BAND_ROWS, vpad.shape[1]), BF16)
        kpad[BAND_ROWS:BAND_ROWS + s_len, :] = k_ref[0]
        vpad[BAND_ROWS:BAND_ROWS + s_len, :] = v_ref[0]
        kt_ref[0, n_keep - n_new:, :] = k_ref[0, s_len - n_new:, :].astype(F32)
        vt_ref[0, n_keep - n_new:, :] = v_ref[0, s_len - n_new:, :].astype(F32)

    def chunks(masked):
        for u in range(n_sub):
            _attn_chunk(q_ref, kpad, vpad, bias_ref, g_ref, o_ref, step * n_sub + u, u, n_pairs, masked)

    if has_hist:
        chunks(masked=False)
    else:
        first_steps = BAND_ROWS // CHUNK // n_sub
        pl.when(step < first_steps)(functools.partial(chunks, masked=True))
        pl.when(step >= first_steps)(functools.partial(chunks, masked=False))


def _attn_chunk(q_ref, kpad, vpad, bias_ref, g_ref, o_ref, c, u, n_pairs, masked):
    start = pl.multiple_of(c * CHUNK, CHUNK)
    rows = slice(u * CHUNK, (u + 1) * CHUNK)
    lane = lax.broadcasted_iota(jnp.int32, (CHUNK, LANES), 1)
    row2 = lax.broadcasted_iota(jnp.int32, (2 * CHUNK, LANES), 0)
    lane2 = lax.broadcasted_iota(jnp.int32, (2 * CHUNK, LANES), 1)
    own_head = (row2 >= CHUNK) == (lane2 >= HEAD_DIM_A)
    if masked:
        col = lax.broadcasted_iota(jnp.int32, (2 * CHUNK, BAND), 1)
        valid = col >= BAND_ROWS - c * CHUNK

    def scores(p):
        cols = slice(p * LANES, (p + 1) * LANES)
        qp = q_ref[0, rows, cols]
        zero = jnp.zeros_like(qp)
        q2 = jnp.concatenate([jnp.where(lane < HEAD_DIM_A, qp, zero),
                              jnp.where(lane >= HEAD_DIM_A, qp, zero)], axis=0)
        return _mm_nt(q2, kpad[pl.ds(start, BAND), cols])

    pairs = range(n_pairs)
    s = [scores(p) + bias_ref[p] for p in pairs]
    if masked:
        s = [jnp.where(valid, s[p], NEG_BIG) for p in pairs]
    m = [jnp.max(s[p], axis=-1, keepdims=True) for p in pairs]
    e = [jnp.exp2((s[p] - m[p]).astype(BF16)) for p in pairs]
    ones = jnp.ones((BAND, LANES), BF16)
    o = [_mm(e[p], jnp.concatenate([vpad[pl.ds(start, BAND), p * LANES:(p + 1) * LANES], ones], axis=1))
         for p in pairs]
    o = [jnp.where(own_head, o[p][:, :LANES] / o[p][:, LANES:], 0.0) for p in pairs]
    ssq = [jnp.sum(o[p] * o[p], axis=-1, keepdims=True) for p in pairs]
    for p in pairs:
        cols = slice(p * LANES, (p + 1) * LANES)
        on = o[p] * lax.rsqrt(ssq[p] * (1.0 / HEAD_DIM_A) + 1e-6)
        o_ref[0, rows, cols] = ((on[:CHUNK] + on[CHUNK:]) * g_ref[:, cols]).astype(o_ref.dtype)


def _attention(z3, hist_k, hist_v, bias_pairs, g_attn):
    b, s_len, _ = z3.shape
    d_attn = g_attn.shape[1]
    n_pairs = d_attn // LANES
    has_hist = hist_k is not None
    n_keep = BAND_ROWS if has_hist else min(BAND_ROWS, s_len)
    n_chunks = s_len // CHUNK
    n_sub = ATTN_CHUNKS_PER_STEP if n_chunks % ATTN_CHUNKS_PER_STEP == 0 else 1
    assert (BAND_ROWS // CHUNK) % n_sub == 0
    in_specs = [pl.BlockSpec((1, n_sub * CHUNK, d_attn), lambda i, c: (i, c, 0)),
                pl.BlockSpec((1, s_len, d_attn), lambda i, c: (i, 0, 1)),
                pl.BlockSpec((1, s_len, d_attn), lambda i, c: (i, 0, 2))]
    args = [z3, z3, z3]
    if has_hist:
        in_specs += [pl.BlockSpec((1, BAND_ROWS, d_attn), lambda i, c: (i, 0, 0))] * 2
        args += [hist_k, hist_v]
    in_specs += [pl.BlockSpec(bias_pairs.shape, lambda i, c: (0, 0, 0)),
                 pl.BlockSpec((1, d_attn), lambda i, c: (0, 0))]
    args += [bias_pairs, g_attn]
    tail_spec = pl.BlockSpec((1, n_keep, d_attn), lambda i, c: (i, 0, 0))
    tail_shape = jax.ShapeDtypeStruct((b, n_keep, d_attn), F32)
    return pl.pallas_call(
        functools.partial(_attn_kernel, has_hist=has_hist, n_pairs=n_pairs, n_sub=n_sub),
        grid=(b, n_chunks // n_sub),
        in_specs=in_specs,
        out_specs=[pl.BlockSpec((1, n_sub * CHUNK, d_attn), lambda i, c: (i, c, 0)), tail_spec, tail_spec],
        out_shape=[jax.ShapeDtypeStruct((b, s_len, d_attn), BF16), tail_shape, tail_shape],
        scratch_shapes=[pltpu.VMEM((BAND_ROWS + s_len, d_attn), BF16)] * 2,
        compiler_params=_params("parallel", "arbitrary"),
        name="attn_hist" if has_hist else "attn",
    )(*args)


def _bias_pairs(rel_bias):
    n_heads = rel_bias.shape[0]
    n_far = BAND - REL_CLIP
    far = jnp.broadcast_to(rel_bias[:, 2 * REL_CLIP:], (n_heads, n_far))
    near = rel_bias[:, REL_CLIP - (CHUNK - 1):2 * REL_CLIP][:, ::-1]
    ext = jnp.concatenate([far, near], axis=1)
    n_ext = BAND + CHUNK - 1
    flat = jnp.tile(jnp.pad(ext, ((0, 0), (0, 1))), (1, CHUNK))
    bias = flat[:, CHUNK - 1:CHUNK - 1 + CHUNK * n_ext].reshape(n_heads, CHUNK, n_ext)[:, :, :BAND]
    return (bias * LOG2_E).reshape(n_heads // 2, 2 * CHUNK, BAND).astype(F32)


MLSTM_BLK = 128
STATE_ROWS = HEAD_DIM_B + 16


def _sigmoid(x):
    return 0.5 * jnp.tanh(0.5 * x) + 0.5


def _log_sigmoid(x):
    return jnp.minimum(x, 0.0) - jnp.log1p(jnp.exp(-jnp.abs(x)))


def _mm_tn(a, b):
    return lax.dot_general(a, b, (((0,), (0,)), ((), ())), preferred_element_type=F32)


def _mlstm_kernel(q_ref, k_ref, v_ref, og_ref, gates_ref, conv0_ref, c0_ref, n0_ref, m0_ref,
                  wconv_ref, bconv_ref, gbias_ref, gnorm_ref,
                  o_ref, c_out_ref, n_out_ref, m_out_ref,
                  xs, qb_s, k_s, ct_s, m_s, *, n_heads, valid_len):
    blk = MLSTM_BLK
    hd = HEAD_DIM_B
    c = pl.program_id(1)
    last = pl.num_programs(1) - 1
    d_b = n_heads * hd
    pad = SUBLANES
    heads = range(n_heads)

    @pl.when(c == 0)
    def _():
        xs[0:pad, :] = conv0_ref[0]
        for h in heads:
            ct_s[h, 0:hd, :] = c0_ref[0, h].T
            ct_s[h, hd:, :] = jnp.zeros((STATE_ROWS - hd, hd), F32)
            ct_s[h, hd:hd + 1, :] = n0_ref[0, h:h + 1, :]
        m_s[...] = m0_ref[0]

    def rows_to_blk(a):
        if valid_len == blk:
            return a
        return jnp.concatenate([a, jnp.zeros((blk - valid_len,) + a.shape[1:], a.dtype)], axis=0)

    xs[pad:pad + blk, 0:d_b] = rows_to_blk(q_ref[0].astype(F32))
    xs[pad:pad + blk, d_b:2 * d_b] = rows_to_blk(k_ref[0].astype(F32))

    gates = rows_to_blk(gates_ref[0]) + gbias_ref[...]
    ig = gates[:, :LANES]
    lf = _log_sigmoid(gates[:, LANES:])
    if valid_len < blk:
        live = lax.broadcasted_iota(jnp.int32, (blk, LANES), 0) < valid_len
        ig = jnp.where(live, ig, NEG_BIG)
        lf = jnp.where(live, lf, 0.0)
    row_i = lax.broadcasted_iota(jnp.int32, (blk, blk), 0)
    col_i = lax.broadcasted_iota(jnp.int32, (blk, blk), 1)
    f_cum = jnp.dot(jnp.where(row_i >= col_i, 1.0, 0.0).astype(F32), lf,
                    precision=lax.Precision.HIGHEST, preferred_element_type=F32)
    sel_rows = -(-n_heads // SUBLANES) * SUBLANES
    sel = jnp.where(lax.broadcasted_iota(jnp.int32, (sel_rows, LANES), 0)
                    == lax.broadcasted_iota(jnp.int32, (sel_rows, LANES), 1), 1.0, 0.0).astype(F32)
    f_rows = _mm_nt(sel, f_cum, precision=lax.Precision.HIGHEST)
    imf = ig - f_cum
    m_prev = m_s[...]
    f_last = f_cum[blk - 1:blk, :]
    g = f_last + imf
    a_last = f_last + m_prev
    m_new = jnp.maximum(a_last, jnp.max(g, axis=0, keepdims=True))
    w_old = jnp.exp(a_last - m_new)
    wk = jnp.exp(g - m_new)

    for j in range(2 * n_heads):
        cols = slice(j * hd, (j + 1) * hd)
        acc = bconv_ref[:, cols]
        for tap in range(CONV_W):
            acc = acc + xs[pl.ds(pad - (CONV_W - 1) + tap, blk), cols] * wconv_ref[tap:tap + 1, cols]
        y = acc * _sigmoid(acc)
        if j < n_heads:
            qb_s[:, cols] = y.astype(BF16)
        else:
            k_s[:, (j - n_heads) * hd:(j - n_heads + 1) * hd] = y * (hd ** -0.5)

    hcols = [slice(h * hd, (h + 1) * hd) for h in heads]
    st = [_mm_nt(k_s[:, hcols[h]].astype(BF16), qb_s[:, hcols[h]]) for h in heads]
    inter = [_mm_nt(ct_s[h].astype(BF16), qb_s[:, hcols[h]]) for h in heads]

    visible = row_i <= col_i
    f_row = [f_rows[h:h + 1, :] for h in heads]
    a_row = [f_row[h] + m_prev[:, h:h + 1] for h in heads]
    dt = [jnp.where(visible, jnp.broadcast_to(imf[:, h:h + 1], (blk, blk)) + f_row[h], NEG_BIG) for h in heads]
    m_t = [jnp.maximum(a_row[h], jnp.max(dt[h], axis=0, keepdims=True)) for h in heads]
    pt = [st[h] * jnp.exp(dt[h] - m_t[h]) for h in heads]
    w_inter = [jnp.exp(a_row[h] - m_t[h]) for h in heads]
    den = [jnp.sum(pt[h], axis=0, keepdims=True) + w_inter[h] * inter[h][hd:hd + 1, :] for h in heads]
    scale = [1.0 / jnp.maximum(jnp.abs(den[h]), jnp.exp(-m_t[h])) for h in heads]
    kw = [k_s[:, hcols[h]] * jnp.broadcast_to(wk[:, h:h + 1], (blk, hd)) for h in heads]

    v = [rows_to_blk(v_ref[0, :, hcols[h]]) for h in heads]
    num_t = [_mm_tn(v[h], pt[h].astype(BF16)) for h in heads]
    dct = [_mm_tn(v[h], kw[h].astype(BF16)) for h in heads]

    ht = [(num_t[h] + w_inter[h] * inter[h][0:hd, :]) * scale[h] for h in heads]
    ssq = [jnp.sum(ht[h] * ht[h], axis=0, keepdims=True) for h in heads]
    for h in heads:
        hn = (ht[h] * lax.rsqrt(ssq[h] * (1.0 / hd) + 1e-6)).T[0:valid_len]
        out = hn * gnorm_ref[:, hcols[h]] * _sigmoid(og_ref[0, :, hcols[h]].astype(F32))
        o_ref[0, :, hcols[h]] = out.astype(o_ref.dtype)
        decay = w_old[:, h:h + 1]
        ct_s[h, 0:hd, :] = decay * ct_s[h, 0:hd, :] + dct[h]
        ct_s[h, hd:hd + 1, :] = decay * ct_s[h, hd:hd + 1, :] + jnp.sum(kw[h], axis=0, keepdims=True)
    m_s[...] = m_new
    xs[0:pad, :] = xs[blk:blk + pad, :]

    @pl.when(c == last)
    def _():
        for h in heads:
            c_out_ref[0, h] = ct_s[h, 0:hd, :].T
            n_out_ref[0, h:h + 1, :] = ct_s[h, hd:hd + 1, :]
        m_out_ref[0] = m_s[...]


def _mlstm(z3, gates3, conv0, c0, n0, m0, w_conv, b_conv, gate_bias, g_norm, col_block0):
    b, s_len, _ = z3.shape
    n_heads = c0.shape[1]
    d_b = n_heads * HEAD_DIM_B
    rows = min(s_len, MLSTM_BLK)
    assert s_len % rows == 0 and rows % (2 * SUBLANES) == 0 and n_heads <= LANES
    zspec = lambda k: pl.BlockSpec((1, rows, d_b), lambda i, c: (i, c, col_block0 + k))
    full = lambda a: pl.BlockSpec(a.shape, lambda i, c: (0,) * a.ndim)
    per_b = lambda a: pl.BlockSpec((1,) + a.shape[1:], lambda i, c: (i,) + (0,) * (a.ndim - 1))
    blk = MLSTM_BLK
    return pl.pallas_call(
        functools.partial(_mlstm_kernel, n_heads=n_heads, valid_len=rows),
        grid=(b, s_len // rows),
        in_specs=[zspec(0), zspec(1), zspec(2), zspec(3),
                  pl.BlockSpec((1, rows, 2 * LANES), lambda i, c: (i, c, 0)),
                  per_b(conv0), per_b(c0), per_b(n0), per_b(m0),
                  full(w_conv), full(b_conv), full(gate_bias), full(g_norm)],
        out_specs=[pl.BlockSpec((1, rows, d_b), lambda i, c: (i, c, 0)),
                   per_b(c0), per_b(n0), per_b(m0)],
        out_shape=[jax.ShapeDtypeStruct((b, s_len, d_b), BF16),
                   jax.ShapeDtypeStruct(c0.shape, F32),
                   jax.ShapeDtypeStruct(n0.shape, F32),
                   jax.ShapeDtypeStruct(m0.shape, F32)],
        scratch_shapes=[pltpu.VMEM((blk + SUBLANES, 2 * d_b), F32),
                        pltpu.VMEM((blk, d_b), BF16),
                        pltpu.VMEM((blk, d_b), F32),
                        pltpu.VMEM((n_heads, STATE_ROWS, HEAD_DIM_B), F32),
                        pltpu.VMEM((1, LANES), F32)],
        compiler_params=_params("parallel", "arbitrary"),
        name="mlstm",
    )(z3, z3, z3, z3, gates3, conv0, c0, n0, m0, w_conv, b_conv, gate_bias, g_norm)


def _layer_norm(r, g, b):
    mu = jnp.mean(r, axis=-1, keepdims=True)
    rc = r - mu
    var = jnp.mean(rc * rc, axis=-1, keepdims=True)
    return rc * lax.rsqrt(var + 1e-5) * g + b


def _row_splits(n_rows):
    sub = n_rows // LN_SPLITS if n_rows % (LN_SPLITS * 2 * SUBLANES) == 0 else n_rows
    return [slice(r, r + sub) for r in range(0, n_rows, sub)]


def _out_proj_kernel(a_ref, m_ref, x_ref, wa_ref, wm_ref, g_ref, b_ref, x1_ref, x1b_ref, *, alpha):
    for rows in _row_splits(x_ref.shape[0]):
        mix = _mm(a_ref[rows, :], wa_ref[...]) + _mm(m_ref[rows, :], wm_ref[...])
        y = _layer_norm(alpha * x_ref[rows, :] + mix, g_ref[...], b_ref[...])
        x1_ref[rows, :] = y
        x1b_ref[rows, :] = y.astype(BF16)


def _out_proj(attn2d, mlstm2d, x2d, w_out, ln_g, ln_b, alpha, bm):
    t, d = x2d.shape
    da, db = attn2d.shape[1], mlstm2d.shape[1]
    assert da == db
    row = lambda w: pl.BlockSpec((bm, w), lambda i: (i, 0))
    vec = pl.BlockSpec((1, d), lambda i: (0, 0))
    return pl.pallas_call(
        functools.partial(_out_proj_kernel, alpha=alpha),
        grid=(t // bm,),
        in_specs=[row(da), row(db), row(d),
                  pl.BlockSpec((da, d), lambda i: (0, 0)),
                  pl.BlockSpec((db, d), lambda i: (1, 0)),
                  vec, vec],
        out_specs=[row(d), row(d)],
        out_shape=[jax.ShapeDtypeStruct((t, d), F32), jax.ShapeDtypeStruct((t, d), BF16)],
        compiler_params=_params("parallel"),
        name="out_proj_ln",
    )(attn2d, mlstm2d, x2d, w_out, w_out, ln_g, ln_b)


def _ffn_up_kernel(x_ref, wg_ref, wu_ref, h_ref, wgb_ref, wub_ref):
    @pl.when(pl.program_id(1) == 0)
    def _():
        wgb_ref[...] = wg_ref[...].astype(BF16)
        wub_ref[...] = wu_ref[...].astype(BF16)

    x = x_ref[...]
    gate = _mm(x, wgb_ref[...])
    h_ref[...] = (gate * _sigmoid(gate) * _mm(x, wub_ref[...])).astype(h_ref.dtype)


def _ffn_up(x1b, w_gate, w_up, bm, bf):
    t, d = x1b.shape
    f = w_gate.shape[1]
    wspec = pl.BlockSpec((d, bf), lambda j, i: (0, j))
    return pl.pallas_call(
        _ffn_up_kernel,
        grid=(f // bf, t // bm),
        in_specs=[pl.BlockSpec((bm, d), lambda j, i: (i, 0)), wspec, wspec],
        out_specs=pl.BlockSpec((bm, bf), lambda j, i: (i, j)),
        out_shape=jax.ShapeDtypeStruct((t, f), BF16),
        scratch_shapes=[pltpu.VMEM((d, bf), BF16)] * 2,
        compiler_params=_params("parallel", "arbitrary"),
        name="ffn_up",
    )(x1b, w_gate, w_up)


def _ffn_down_kernel(h_ref, wd_ref, x1_ref, g_ref, b_ref, y_ref, *, alpha):
    for rows in _row_splits(x1_ref.shape[0]):
        y_ref[rows, :] = _layer_norm(alpha * x1_ref[rows, :] + _mm(h_ref[rows, :], wd_ref[...]),
                                     g_ref[...], b_ref[...])


def _ffn_down(h, w_down, x1, ln_g, ln_b, alpha, bm):
    t, f = h.shape
    d = w_down.shape[1]
    vec = pl.BlockSpec((1, d), lambda i: (0, 0))
    return pl.pallas_call(
        functools.partial(_ffn_down_kernel, alpha=alpha),
        grid=(t // bm,),
        in_specs=[pl.BlockSpec((bm, f), lambda i: (i, 0)),
                  pl.BlockSpec((f, d), lambda i: (0, 0), pipeline_mode=pl.Buffered(1)),
                  pl.BlockSpec((bm, d), lambda i: (i, 0)),
                  vec, vec],
        out_specs=pl.BlockSpec((bm, d), lambda i: (i, 0)),
        out_shape=jax.ShapeDtypeStruct((t, d), F32),
        compiler_params=_params("parallel"),
        name="ffn_down_ln",
    )(h, w_down, x1, ln_g, ln_b)


def _largest_divisor(n, cap):
    return max(d for d in range(1, min(n, cap) + 1) if n % d == 0)


def _trunk_layer(x, hist_k, hist_v, conv_state, mlstm_state, w, alpha):
    b, s_len, d = x.shape
    t = b * s_len
    c0, n0, m0 = mlstm_state
    n_heads_b = c0.shape[1]
    d_b = n_heads_b * HEAD_DIM_B
    d_attn = w["g_attn"].shape[1]
    n_main = 3 * d_attn + 4 * d_b
    assert d_attn == d_b and w["w_main_t"].shape[0] == n_main

    bm = _largest_divisor(t, 1024)
    x2d = x.reshape(t, d)
    z, gates = _in_proj(x2d, w["w_main_t"], w["w_gate_t"], bm, _largest_divisor(n_main // LANES, 14) * LANES)
    z3 = z.reshape(b, s_len, -1)

    attn, k_tail, v_tail = _attention(z3, hist_k, hist_v, w["bias_pairs"], w["g_attn"])

    conv0 = jnp.pad(conv_state.astype(F32), ((0, 0), (SUBLANES - (CONV_W - 1), 0), (0, 0)))
    m0_row = jnp.pad(m0.astype(F32)[:, None, :], ((0, 0), (0, 0), (0, LANES - n_heads_b)))
    mlstm, c_new, n_new, m_new = _mlstm(
        z3, gates.reshape(b, s_len, 2 * LANES), conv0, c0.astype(F32), n0.astype(F32), m0_row,
        w["w_conv"], w["b_conv"], w["gate_bias"], w["g_mlstm"], 3)

    bm2 = _largest_divisor(t, 512)
    x1, x1b = _out_proj(attn.reshape(t, d_attn), mlstm.reshape(t, d_b), x2d, w["w_out"],
                        w["ln1_g"], w["ln1_b"], alpha, bm2)
    d_ff = w["w_ffn_gate"].shape[1]
    hmid = _ffn_up(x1b, w["w_ffn_gate"], w["w_ffn_up"], bm, _largest_divisor(d_ff // MXU_COLS, 2) * MXU_COLS)
    y = _ffn_down(hmid, w["w_ffn_down"], x1, w["ln2_g"], w["ln2_b"], alpha, bm2)

    n_heads_a = d_attn // HEAD_DIM_A
    to_heads = lambda a: a.reshape(a.shape[0], a.shape[1], n_heads_a, HEAD_DIM_A)
    qk_raw_tail = z3[:, s_len - (CONV_W - 1):, 3 * d_attn:3 * d_attn + 2 * d_b].astype(F32)
    return (y.reshape(b, s_len, d), to_heads(k_tail), to_heads(v_tail), qk_raw_tail,
            c_new, n_new, m_new[:, 0, :n_heads_b])


def kernel(x_prompt, x_sample, cache_attn_k, cache_attn_v, state_conv, state_mlstm_C, state_mlstm_n, state_mlstm_m, w_in, b_igate, b_fgate, w_conv, b_conv, rel_bias, g_attn_norm, g_mlstm_norm, w_out, ln1_g, ln1_b, w_ffn_gate, w_ffn_up, w_ffn_down, ln2_g, ln2_b):
    depth = w_in.shape[0]
    alpha = (2.0 * depth) ** 0.25
    xp, xs = x_prompt, x_sample
    bp, sp, d_model = xp.shape
    n_heads_b = b_igate.shape[1]
    d_b = n_heads_b * HEAD_DIM_B
    d_attn = g_attn_norm.shape[1]
    n_heads_a = d_attn // HEAD_DIM_A
    n_main = 3 * d_attn + 4 * d_b
    new_p, new_s = [], []
    for l in range(depth):
        q_scale = jnp.where(jnp.arange(n_main) < d_attn, (HEAD_DIM_A ** -0.5) * LOG2_E, 1.0).astype(F32)
        w_in_t = jnp.swapaxes(w_in[l], 0, 1)
        lane_pad = lambda a: jnp.pad(a, ((0, 0), (0, LANES - n_heads_b)))
        row_pad = lambda a: jnp.pad(a, ((0, LANES - n_heads_b), (0, 0)))
        w = dict(
            w_main_t=(w_in_t[:n_main] * q_scale[:, None]).astype(BF16),
            w_gate_t=jnp.concatenate([row_pad(w_in_t[n_main:n_main + n_heads_b]),
                                      row_pad(w_in_t[n_main + n_heads_b:])], axis=0),
            gate_bias=jnp.concatenate([lane_pad(b_igate[l][None, :]), lane_pad(b_fgate[l][None, :])],
                                      axis=1).astype(F32),
            w_conv=w_conv[l], b_conv=b_conv[l][None, :],
            bias_pairs=_bias_pairs(rel_bias[l]),
            g_attn=g_attn_norm[l][None, :], g_mlstm=g_mlstm_norm[l][None, :],
            w_out=w_out[l].astype(BF16), ln1_g=ln1_g[l][None, :], ln1_b=ln1_b[l][None, :],
            w_ffn_gate=w_ffn_gate[l], w_ffn_up=w_ffn_up[l],
            w_ffn_down=w_ffn_down[l].astype(BF16), ln2_g=ln2_g[l][None, :], ln2_b=ln2_b[l][None, :],
        )
        zero_state = (jnp.zeros((bp, n_heads_b, HEAD_DIM_B, HEAD_DIM_B), F32),
                      jnp.zeros((bp, n_heads_b, HEAD_DIM_B), F32),
                      jnp.zeros((bp, n_heads_b), F32))
        zero_conv = jnp.zeros((bp, CONV_W - 1, 2 * d_b), F32)
        xp, *state_p = _trunk_layer(xp, None, None, zero_conv, zero_state, w, alpha)
        new_p.append(tuple(state_p))
        bs = xs.shape[0]
        ck = cache_attn_k[l].astype(F32).reshape(bs, -1, d_attn)
        cv = cache_attn_v[l].astype(F32).reshape(bs, -1, d_attn)
        assert ck.shape[1] == BAND_ROWS
        xs, *state_s = _trunk_layer(xs, ck, cv, state_conv[l],
                                    (state_mlstm_C[l], state_mlstm_n[l], state_mlstm_m[l]), w, alpha)
        new_s.append(tuple(state_s))
    k_p, v_p, conv_p, C_p, n_p, m_p = [jnp.stack(t) for t in zip(*new_p)]
    k_s, v_s, conv_s, C_s, n_s, m_s = [jnp.stack(t) for t in zip(*new_s)]
    return (xp, xs, k_p, v_p, conv_p, C_p, n_p, m_p, k_s, v_s, conv_s, C_s, n_s, m_s)
```

```python
import functools

import jax
import jax.numpy as jnp
from jax import lax
from jax.experimental import pallas as pl
from jax.experimental.pallas import tpu as pltpu

CHUNK = 64
BAND_ROWS = 8 * CHUNK
BAND = BAND_ROWS + CHUNK
REL_CLIP = 256
HEAD_DIM_A = 64
HEAD_DIM_B = 128
CONV_W = 4
LANES = 128
SUBLANES = 8
MXU_COLS = 256
NEG_BIG = -1e30
LOG2_E = 1.4426950408889634
LN_SPLITS = 4
ATTN_CHUNKS_PER_STEP = 8
VMEM_LIMIT_BYTES = 56 * 1024 * 1024

F32 = jnp.float32
BF16 = jnp.bfloat16


def _params(*semantics, flags=None):
    return pltpu.CompilerParams(dimension_semantics=semantics, vmem_limit_bytes=VMEM_LIMIT_BYTES,
                                flags=flags)


def _mm(a, b):
    return jnp.dot(a, b, preferred_element_type=F32)


def _mm_nt(a, b, precision=None):
    return lax.dot_general(a, b, (((1,), (1,)), ((), ())), precision=precision,
                           preferred_element_type=F32)


def _in_proj_kernel(x_ref, wt_ref, wgt_ref, z_ref, g_ref, xb_ref):
    @pl.when(pl.program_id(1) == 0)
    def _():
        xb = x_ref[...].astype(BF16)
        xb_ref[...] = xb
        g_ref[...] = _mm_nt(xb, wgt_ref[...].astype(BF16))

    z_ref[...] = _mm_nt(xb_ref[...], wt_ref[...]).astype(z_ref.dtype)


def _in_proj(x2d, w_main_t, w_gate_t, bm, bn):
    t, d = x2d.shape
    n = w_main_t.shape[0]
    ng = w_gate_t.shape[0]
    return pl.pallas_call(
        _in_proj_kernel,
        grid=(t // bm, n // bn),
        in_specs=[pl.BlockSpec((bm, d), lambda i, j: (i, 0)),
                  pl.BlockSpec((bn, d), lambda i, j: (j, 0)),
                  pl.BlockSpec((ng, d), lambda i, j: (0, 0))],
        out_specs=[pl.BlockSpec((bm, bn), lambda i, j: (i, j)),
                   pl.BlockSpec((bm, ng), lambda i, j: (i, 0))],
        out_shape=[jax.ShapeDtypeStruct((t, n), BF16),
                   jax.ShapeDtypeStruct((t, ng), F32)],
        scratch_shapes=[pltpu.VMEM((bm, d), BF16)],
        compiler_params=_params("parallel", "arbitrary"),
        name="in_proj",
    )(x2d, w_main_t, w_gate_t)


def _attn_kernel(*refs, has_hist, n_pairs, n_sub):
    if has_hist:
        q_ref, k_ref, v_ref, kh_ref, vh_ref, bias_ref, g_ref, o_ref, kt_ref, vt_ref, kpad, vpad = refs
    else:
        q_ref, k_ref, v_ref, bias_ref, g_ref, o_ref, kt_ref, vt_ref, kpad, vpad = refs
    step = pl.program_id(1)
    s_len = k_ref.shape[1]
    n_keep = kt_ref.shape[1]
    n_new = min(n_keep, s_len)

    @pl.when(step == 0)
    def _():
        if has_hist:
            kpad[0:BAND_ROWS, :] = kh_ref[0].astype(BF16)
            vpad[0:BAND_ROWS, :] = vh_ref[0].astype(BF16)
            if n_new < n_keep:
                kt_ref[0, 0:n_keep - n_new, :] = kh_ref[0, BAND_ROWS - (n_keep - n_new):, :]
                vt_ref[0, 0:n_keep - n_new, :] = vh_ref[0, BAND_ROWS - (n_keep - n_new):, :]
        else:
            kpad[0:BAND_ROWS, :] = jnp.zeros((BAND_ROWS, kpad.shape[1]), BF16)
            vpad[0:BAND_ROWS, :] = jnp.zeros((BAND_ROWS, vpad.shape[1]), BF16)
        kpad[BAND_ROWS:BAND_ROWS + s_len, :] = k_ref[0]
        vpad[BAND_ROWS:BAND_ROWS + s_len, :] = v_ref[0]
        kt_ref[0, n_keep - n_new:, :] = k_ref[0, s_len - n_new:, :].astype(F32)
        vt_ref[0, n_keep - n_new:, :] = v_ref[0, s_len - n_new:, :].astype(F32)

    def chunks(masked):
        for u in range(n_sub):
            _attn_chunk(q_ref, kpad, vpad, bias_ref, g_ref, o_ref, step * n_sub + u, u, n_pairs, masked)

    if has_hist:
        chunks(masked=False)
    else:
        first_steps = BAND_ROWS // CHUNK // n_sub
        pl.when(step < first_steps)(functools.partial(chunks, masked=True))
        pl.when(step >= first_steps)(functools.partial(chunks, masked=False))


def _attn_chunk(q_ref, kpad, vpad, bias_ref, g_ref, o_ref, c, u, n_pairs, masked):
    start = pl.multiple_of(c * CHUNK, CHUNK)
    rows = slice(u * CHUNK, (u + 1) * CHUNK)
    lane = lax.broadcasted_iota(jnp.int32, (CHUNK, LANES), 1)
    row2 = lax.broadcasted_iota(jnp.int32, (2 * CHUNK, LANES), 0)
    lane2 = lax.broadcasted_iota(jnp.int32, (2 * CHUNK, LANES), 1)
    own_head = (row2 >= CHUNK) == (lane2 >= HEAD_DIM_A)
    if masked:
        col = lax.broadcasted_iota(jnp.int32, (2 * CHUNK, BAND), 1)
        valid = col >= BAND_ROWS - c * CHUNK

    def scores(p):
        cols = slice(p * LANES, (p + 1) * LANES)
        qp = q_ref[0, rows, cols]
        zero = jnp.zeros_like(qp)
        q2 = jnp.concatenate([jnp.where(lane < HEAD_DIM_A, qp, zero),
                              jnp.where(lane >= HEAD_DIM_A, qp, zero)], axis=0)
        return _mm_nt(q2, kpad[pl.ds(start, BAND), cols])

    pairs = range(n_pairs)
    s = [scores(p) + bias_ref[p] for p in pairs]
    if masked:
        s = [jnp.where(valid, s[p], NEG_BIG) for p in pairs]
    m = [jnp.max(s[p], axis=-1, keepdims=True) for p in pairs]
    e = [jnp.exp2((s[p] - m[p]).astype(BF16)) for p in pairs]
    ones = jnp.ones((BAND, LANES), BF16)
    o = [_mm(e[p], jnp.concatenate([vpad[pl.ds(start, BAND), p * LANES:(p + 1) * LANES], ones], axis=1))
         for p in pairs]
    o = [jnp.where(own_head, o[p][:, :LANES] / o[p][:, LANES:], 0.0) for p in pairs]
    ssq = [jnp.sum(o[p] * o[p], axis=-1, keepdims=True) for p in pairs]
    for p in pairs:
        cols = slice(p * LANES, (p + 1) * LANES)
        on = o[p] * lax.rsqrt(ssq[p] * (1.0 / HEAD_DIM_A) + 1e-6)
        o_ref[0, rows, cols] = ((on[:CHUNK] + on[CHUNK:]) * g_ref[:, cols]).astype(o_ref.dtype)


def _attention(z3, hist_k, hist_v, bias_pairs, g_attn):
    b, s_len, _ = z3.shape
    d_attn = g_attn.shape[1]
    n_pairs = d_attn // LANES
    has_hist = hist_k is not None
    n_keep = BAND_ROWS if has_hist else min(BAND_ROWS, s_len)
    n_chunks = s_len // CHUNK
    n_sub = ATTN_CHUNKS_PER_STEP if n_chunks % ATTN_CHUNKS_PER_STEP == 0 else 1
    assert (BAND_ROWS // CHUNK) % n_sub == 0
    in_specs = [pl.BlockSpec((1, n_sub * CHUNK, d_attn), lambda i, c: (i, c, 0)),
                pl.BlockSpec((1, s_len, d_attn), lambda i, c: (i, 0, 1)),
                pl.BlockSpec((1, s_len, d_attn), lambda i, c: (i, 0, 2))]
    args = [z3, z3, z3]
    if has_hist:
        in_specs += [pl.BlockSpec((1, BAND_ROWS, d_attn), lambda i, c: (i, 0, 0))] * 2
        args += [hist_k, hist_v]
    in_specs += [pl.BlockSpec(bias_pairs.shape, lambda i, c: (0, 0, 0)),
                 pl.BlockSpec((1, d_attn), lambda i, c: (0, 0))]
    args += [bias_pairs, g_attn]
    tail_spec = pl.BlockSpec((1, n_keep, d_attn), lambda i, c: (i, 0, 0))
    tail_shape = jax.ShapeDtypeStruct((b, n_keep, d_attn), F32)
    return pl.pallas_call(
        functools.partial(_attn_kernel, has_hist=has_hist, n_pairs=n_pairs, n_sub=n_sub),
        grid=(b, n_chunks // n_sub),
        in_specs=in_specs,
        out_specs=[pl.BlockSpec((1, n_sub * CHUNK, d_attn), lambda i, c: (i, c, 0)), tail_spec, tail_spec],
        out_shape=[jax.ShapeDtypeStruct((b, s_len, d_attn), BF16), tail_shape, tail_shape],
        scratch_shapes=[pltpu.VMEM((BAND_ROWS + s_len, d_attn), BF16)] * 2,
        compiler_params=_params("parallel", "arbitrary"),
        name="attn_hist" if has_hist else "attn",
    )(*args)


def _bias_pairs(rel_bias):
    n_heads = rel_bias.shape[0]
    n_far = BAND - REL_CLIP
    far = jnp.broadcast_to(rel_bias[:, 2 * REL_CLIP:], (n_heads, n_far))
    near = rel_bias[:, REL_CLIP - (CHUNK - 1):2 * REL_CLIP][:, ::-1]
    ext = jnp.concatenate([far, near], axis=1)
    n_ext = BAND + CHUNK - 1
    flat = jnp.tile(jnp.pad(ext, ((0, 0), (0, 1))), (1, CHUNK))
    bias = flat[:, CHUNK - 1:CHUNK - 1 + CHUNK * n_ext].reshape(n_heads, CHUNK, n_ext)[:, :, :BAND]
    return (bias * LOG2_E).reshape(n_heads // 2, 2 * CHUNK, BAND).astype(F32)


MLSTM_BLK = 128
STATE_ROWS = HEAD_DIM_B + 16


def _sigmoid(x):
    return 0.5 * jnp.tanh(0.5 * x) + 0.5


def _log_sigmoid(x):
    return jnp.minimum(x, 0.0) - jnp.log1p(jnp.exp(-jnp.abs(x)))


def _mm_tn(a, b):
    return lax.dot_general(a, b, (((0,), (0,)), ((), ())), preferred_element_type=F32)


def _mlstm_kernel(q_ref, k_ref, v_ref, og_ref, gates_ref, conv0_ref, c0_ref, n0_ref, m0_ref,
                  wconv_ref, bconv_ref, gbias_ref, gnorm_ref,
                  o_ref, c_out_ref, n_out_ref, m_out_ref,
                  xs, qb_s, k_s, ct_s, m_s, *, n_heads, valid_len):
    blk = MLSTM_BLK
    hd = HEAD_DIM_B
    c = pl.program_id(1)
    last = pl.num_programs(1) - 1
    d_b = n_heads * hd
    pad = SUBLANES
    heads = range(n_heads)

    @pl.when(c == 0)
    def _():
        xs[0:pad, :] = conv0_ref[0]
        for h in heads:
            ct_s[h, 0:hd, :] = c0_ref[0, h].T
            ct_s[h, hd:, :] = jnp.zeros((STATE_ROWS - hd, hd), F32)
            ct_s[h, hd:hd + 1, :] = n0_ref[0, h:h + 1, :]
        m_s[...] = m0_ref[0]

    def rows_to_blk(a):
        if valid_len == blk:
            return a
        return jnp.concatenate([a, jnp.zeros((blk - valid_len,) + a.shape[1:], a.dtype)], axis=0)

    xs[pad:pad + blk, 0:d_b] = rows_to_blk(q_ref[0].astype(F32))
    xs[pad:pad + blk, d_b:2 * d_b] = rows_to_blk(k_ref[0].astype(F32))

    gates = rows_to_blk(gates_ref[0]) + gbias_ref[...]
    ig = gates[:, :LANES]
    lf = _log_sigmoid(gates[:, LANES:])
    if valid_len < blk:
        live = lax.broadcasted_iota(jnp.int32, (blk, LANES), 0) < valid_len
        ig = jnp.where(live, ig, NEG_BIG)
        lf = jnp.where(live, lf, 0.0)
    row_i = lax.broadcasted_iota(jnp.int32, (blk, blk), 0)
    col_i = lax.broadcasted_iota(jnp.int32, (blk, blk), 1)
    f_cum = jnp.dot(jnp.where(row_i >= col_i, 1.0, 0.0).astype(F32), lf,
                    precision=lax.Precision.HIGHEST, preferred_element_type=F32)
    sel_rows = -(-n_heads // SUBLANES) * SUBLANES
    sel = jnp.where(lax.broadcasted_iota(jnp.int32, (sel_rows, LANES), 0)
                    == lax.broadcasted_iota(jnp.int32, (sel_rows, LANES), 1), 1.0, 0.0).astype(F32)
    f_rows = _mm_nt(sel, f_cum, precision=lax.Precision.HIGHEST)
    imf = ig - f_cum
    m_prev = m_s[...]
    f_last = f_cum[blk - 1:blk, :]
    g = f_last + imf
    a_last = f_last + m_prev
    m_new = jnp.maximum(a_last, jnp.max(g, axis=0, keepdims=True))
    w_old = jnp.exp(a_last - m_new)
    wk = jnp.exp(g - m_new)

    for j in range(2 * n_heads):
        cols = slice(j * hd, (j + 1) * hd)
        acc = bconv_ref[:, cols]
        for tap in range(CONV_W):
            acc = acc + xs[pl.ds(pad - (CONV_W - 1) + tap, blk), cols] * wconv_ref[tap:tap + 1, cols]
        y = acc * _sigmoid(acc)
        if j < n_heads:
            qb_s[:, cols] = y.astype(BF16)
        else:
            k_s[:, (j - n_heads) * hd:(j - n_heads + 1) * hd] = y * (hd ** -0.5)

    hcols = [slice(h * hd, (h + 1) * hd) for h in heads]
    st = [_mm_nt(k_s[:, hcols[h]].astype(BF16), qb_s[:, hcols[h]]) for h in heads]
    inter = [_mm_nt(ct_s[h].astype(BF16), qb_s[:, hcols[h]]) for h in heads]

    visible = row_i <= col_i
    f_row = [f_rows[h:h + 1, :] for h in heads]
    a_row = [f_row[h] + m_prev[:, h:h + 1] for h in heads]
    dt = [jnp.where(visible, jnp.broadcast_to(imf[:, h:h + 1], (blk, blk)) + f_row[h], NEG_BIG) for h in heads]
    m_t = [jnp.maximum(a_row[h], jnp.max(dt[h], axis=0, keepdims=True)) for h in heads]
    pt = [st[h] * jnp.exp(dt[h] - m_t[h]) for h in heads]
    w_inter = [jnp.exp(a_row[h] - m_t[h]) for h in heads]
    den = [jnp.sum(pt[h], axis=0, keepdims=True) + w_inter[h] * inter[h][hd:hd + 1, :] for h in heads]
    scale = [1.0 / jnp.maximum(jnp.abs(den[h]), jnp.exp(-m_t[h])) for h in heads]
    kw = [k_s[:, hcols[h]] * jnp.broadcast_to(wk[:, h:h + 1], (blk, hd)) for h in heads]

    v = [rows_to_blk(v_ref[0, :, hcols[h]]) for h in heads]
    num_t = [_mm_tn(v[h], pt[h].astype(BF16)) for h in heads]
    dct = [_mm_tn(v[h], kw[h].astype(BF16)) for h in heads]

    ht = [(num_t[h] + w_inter[h] * inter[h][0:hd, :]) * scale[h] for h in heads]
    ssq = [jnp.sum(ht[h] * ht[h], axis=0, keepdims=True) for h in heads]
    for h in heads:
        hn = (ht[h] * lax.rsqrt(ssq[h] * (1.0 / hd) + 1e-6)).T[0:valid_len]
        out = hn * gnorm_ref[:, hcols[h]] * _sigmoid(og_ref[0, :, hcols[h]].astype(F32))
        o_ref[0, :, hcols[h]] = out.astype(o_ref.dtype)
        decay = w_old[:, h:h + 1]
        ct_s[h, 0:hd, :] = decay * ct_s[h, 0:hd, :] + dct[h]
        ct_s[h, hd:hd + 1, :] = decay * ct_s[h, hd:hd + 1, :] + jnp.sum(kw[h], axis=0, keepdims=True)
    m_s[...] = m_new
    xs[0:pad, :] = xs[blk:blk + pad, :]

    @pl.when(c == last)
    def _():
        for h in heads:
            c_out_ref[0, h] = ct_s[h, 0:hd, :].T
            n_out_ref[0, h:h + 1, :] = ct_s[h, hd:hd + 1, :]
        m_out_ref[0] = m_s[...]


def _mlstm(z3, gates3, conv0, c0, n0, m0, w_conv, b_conv, gate_bias, g_norm, col_block0):
    b, s_len, _ = z3.shape
    n_heads = c0.shape[1]
    d_b = n_heads * HEAD_DIM_B
    rows = min(s_len, MLSTM_BLK)
    assert s_len % rows == 0 and rows % (2 * SUBLANES) == 0 and n_heads <= LANES
    zspec = lambda k: pl.BlockSpec((1, rows, d_b), lambda i, c: (i, c, col_block0 + k))
    full = lambda a: pl.BlockSpec(a.shape, lambda i, c: (0,) * a.ndim)
    per_b = lambda a: pl.BlockSpec((1,) + a.shape[1:], lambda i, c: (i,) + (0,) * (a.ndim - 1))
    blk = MLSTM_BLK
    return pl.pallas_call(
        functools.partial(_mlstm_kernel, n_heads=n_heads, valid_len=rows),
        grid=(b, s_len // rows),
        in_specs=[zspec(0), zspec(1), zspec(2), zspec(3),
                  pl.BlockSpec((1, rows, 2 * LANES), lambda i, c: (i, c, 0)),
                  per_b(conv0), per_b(c0), per_b(n0), per_b(m0),
                  full(w_conv), full(b_conv), full(gate_bias), full(g_norm)],
        out_specs=[pl.BlockSpec((1, rows, d_b), lambda i, c: (i, c, 0)),
                   per_b(c0), per_b(n0), per_b(m0)],
        out_shape=[jax.ShapeDtypeStruct((b, s_len, d_b), BF16),
                   jax.ShapeDtypeStruct(c0.shape, F32),
                   jax.ShapeDtypeStruct(n0.shape, F32),
                   jax.ShapeDtypeStruct(m0.shape, F32)],
        scratch_shapes=[pltpu.VMEM((blk + SUBLANES, 2 * d_b), F32),
                        pltpu.VMEM((blk, d_b), BF16),
                        pltpu.VMEM((blk, d_b), F32),
                        pltpu.VMEM((n_heads, STATE_ROWS, HEAD_DIM_B), F32),
                        pltpu.VMEM((1, LANES), F32)],
        compiler_params=_params("parallel", "arbitrary"),
        name="mlstm",
    )(z3, z3, z3, z3, gates3, conv0, c0, n0, m0, w_conv, b_conv, gate_bias, g_norm)


def _layer_norm(r, g, b):
    mu = jnp.mean(r, axis=-1, keepdims=True)
    rc = r - mu
    var = jnp.mean(rc * rc, axis=-1, keepdims=True)
    return rc * lax.rsqrt(var + 1e-5) * g + b


def _row_splits(n_rows):
    sub = n_rows // LN_SPLITS if n_rows % (LN_SPLITS * 2 * SUBLANES) == 0 else n_rows
    return [slice(r, r + sub) for r in range(0, n_rows, sub)]


def _out_proj_kernel(a_ref, m_ref, x_ref, wa_ref, wm_ref, g_ref, b_ref, x1_ref, x1b_ref, *, alpha):
    for rows in _row_splits(x_ref.shape[0]):
        mix = _mm(a_ref[rows, :], wa_ref[...]) + _mm(m_ref[rows, :], wm_ref[...])
        y = _layer_norm(alpha * x_ref[rows, :] + mix, g_ref[...], b_ref[...])
        x1_ref[rows, :] = y
        x1b_ref[rows, :] = y.astype(BF16)


def _out_proj(attn2d, mlstm2d, x2d, w_out, ln_g, ln_b, alpha, bm):
    t, d = x2d.shape
    da, db = attn2d.shape[1], mlstm2d.shape[1]
    assert da == db
    row = lambda w: pl.BlockSpec((bm, w), lambda i: (i, 0))
    vec = pl.BlockSpec((1, d), lambda i: (0, 0))
    return pl.pallas_call(
        functools.partial(_out_proj_kernel, alpha=alpha),
        grid=(t // bm,),
        in_specs=[row(da), row(db), row(d),
                  pl.BlockSpec((da, d), lambda i: (0, 0)),
                  pl.BlockSpec((db, d), lambda i: (1, 0)),
                  vec, vec],
        out_specs=[row(d), row(d)],
        out_shape=[jax.ShapeDtypeStruct((t, d), F32), jax.ShapeDtypeStruct((t, d), BF16)],
        compiler_params=_params("parallel"),
        name="out_proj_ln",
    )(attn2d, mlstm2d, x2d, w_out, w_out, ln_g, ln_b)


def _ffn_up_kernel(x_ref, wg_ref, wu_ref, h_ref, wgb_ref, wub_ref):
    @pl.when(pl.program_id(1) == 0)
    def _():
        wgb_ref[...] = wg_ref[...].astype(BF16)
        wub_ref[...] = wu_ref[...].astype(BF16)

    x = x_ref[...]
    gate = _mm(x, wgb_ref[...])
    h_ref[...] = (gate * _sigmoid(gate) * _mm(x, wub_ref[...])).astype(h_ref.dtype)


def _ffn_up(x1b, w_gate, w_up, bm, bf):
    t, d = x1b.shape
    f = w_gate.shape[1]
    wspec = pl.BlockSpec((d, bf), lambda j, i: (0, j))
    return pl.pallas_call(
        _ffn_up_kernel,
        grid=(f // bf, t // bm),
        in_specs=[pl.BlockSpec((bm, d), lambda j, i: (i, 0)), wspec, wspec],
        out_specs=pl.BlockSpec((bm, bf), lambda j, i: (i, j)),
        out_shape=jax.ShapeDtypeStruct((t, f), BF16),
        scratch_shapes=[pltpu.VMEM((d, bf), BF16)] * 2,
        compiler_params=_params("parallel", "arbitrary"),
        name="ffn_up",
    )(x1b, w_gate, w_up)


def _ffn_down_kernel(h_ref, wd_ref, x1_ref, g_ref, b_ref, y_ref, *, alpha):
    for rows in _row_splits(x1_ref.shape[0]):
        y_ref[rows, :] = _layer_norm(alpha * x1_ref[rows, :] + _mm(h_ref[rows, :], wd_ref[...]),
                                     g_ref[...], b_ref[...])


def _ffn_down(h, w_down, x1, ln_g, ln_b, alpha, bm):
    t, f = h.shape
    d = w_down.shape[1]
    vec = pl.BlockSpec((1, d), lambda i: (0, 0))
    return pl.pallas_call(
        functools.partial(_ffn_down_kernel, alpha=alpha),
        grid=(t // bm,),
        in_specs=[pl.BlockSpec((bm, f), lambda i: (i, 0)),
                  pl.BlockSpec((f, d), lambda i: (0, 0), pipeline_mode=pl.Buffered(1)),
                  pl.BlockSpec((bm, d), lambda i: (i, 0)),
                  vec, vec],
        out_specs=pl.BlockSpec((bm, d), lambda i: (i, 0)),
        out_shape=jax.ShapeDtypeStruct((t, d), F32),
        compiler_params=_params("parallel"),
        name="ffn_down_ln",
    )(h, w_down, x1, ln_g, ln_b)


def _largest_divisor(n, cap):
    return max(d for d in range(1, min(n, cap) + 1) if n % d == 0)


def _tiles(t, n_main, d_ff):
    return dict(
        rows=_largest_divisor(t, 1024),
        rows_ln=_largest_divisor(t, 512),
        in_proj_cols=_largest_divisor(n_main // MXU_COLS, 7) * MXU_COLS,
        ffn_cols=_largest_divisor(d_ff // MXU_COLS, 2) * MXU_COLS,
    )


def _trunk_layer(x, hist_k, hist_v, conv_state, mlstm_state, w, alpha):
    b, s_len, d = x.shape
    t = b * s_len
    c0, n0, m0 = mlstm_state
    n_heads_b = c0.shape[1]
    d_b = n_heads_b * HEAD_DIM_B
    d_attn = w["g_attn"].shape[1]
    n_main = 3 * d_attn + 4 * d_b
    assert d_attn == d_b and w["w_main_t"].shape[0] == n_main

    tiles = _tiles(t, n_main, w["w_ffn_gate"].shape[1])
    x2d = x.reshape(t, d)
    z, gates = _in_proj(x2d, w["w_main_t"], w["w_gate_t"], tiles["rows"], tiles["in_proj_cols"])
    z3 = z.reshape(b, s_len, -1)

    attn, k_tail, v_tail = _attention(z3, hist_k, hist_v, w["bias_pairs"], w["g_attn"])

    conv0 = jnp.pad(conv_state.astype(F32), ((0, 0), (SUBLANES - (CONV_W - 1), 0), (0, 0)))
    m0_row = jnp.pad(m0.astype(F32)[:, None, :], ((0, 0), (0, 0), (0, LANES - n_heads_b)))
    mlstm, c_new, n_new, m_new = _mlstm(
        z3, gates.reshape(b, s_len, 2 * LANES), conv0, c0.astype(F32), n0.astype(F32), m0_row,
        w["w_conv"], w["b_conv"], w["gate_bias"], w["g_mlstm"], 3)

    x1, x1b = _out_proj(attn.reshape(t, d_attn), mlstm.reshape(t, d_b), x2d, w["w_out"],
                        w["ln1_g"], w["ln1_b"], alpha, tiles["rows_ln"])
    hmid = _ffn_up(x1b, w["w_ffn_gate"], w["w_ffn_up"], tiles["rows"], tiles["ffn_cols"])
    y = _ffn_down(hmid, w["w_ffn_down"], x1, w["ln2_g"], w["ln2_b"], alpha, tiles["rows_ln"])

    n_heads_a = d_attn // HEAD_DIM_A
    to_heads = lambda a: a.reshape(a.shape[0], a.shape[1], n_heads_a, HEAD_DIM_A)
    qk_raw_tail = z3[:, s_len - (CONV_W - 1):, 3 * d_attn:3 * d_attn + 2 * d_b].astype(F32)
    return (y.reshape(b, s_len, d), to_heads(k_tail), to_heads(v_tail), qk_raw_tail,
            c_new, n_new, m_new[:, 0, :n_heads_b])


def kernel(x_prompt, x_sample, cache_attn_k, cache_attn_v, state_conv, state_mlstm_C, state_mlstm_n, state_mlstm_m, w_in, b_igate, b_fgate, w_conv, b_conv, rel_bias, g_attn_norm, g_mlstm_norm, w_out, ln1_g, ln1_b, w_ffn_gate, w_ffn_up, w_ffn_down, ln2_g, ln2_b):
    depth = w_in.shape[0]
    alpha = (2.0 * depth) ** 0.25
    xp, xs = x_prompt, x_sample
    bp, sp, d_model = xp.shape
    n_heads_b = b_igate.shape[1]
    d_b = n_heads_b * HEAD_DIM_B
    d_attn = g_attn_norm.shape[1]
    n_heads_a = d_attn // HEAD_DIM_A
    n_main = 3 * d_attn + 4 * d_b
    new_p, new_s = [], []
    for l in range(depth):
        q_scale = jnp.where(jnp.arange(n_main) < d_attn, (HEAD_DIM_A ** -0.5) * LOG2_E, 1.0).astype(F32)
        w_in_t = jnp.swapaxes(w_in[l], 0, 1)
        lane_pad = lambda a: jnp.pad(a, ((0, 0), (0, LANES - n_heads_b)))
        row_pad = lambda a: jnp.pad(a, ((0, LANES - n_heads_b), (0, 0)))
        w = dict(
            w_main_t=(w_in_t[:n_main] * q_scale[:, None]).astype(BF16),
            w_gate_t=jnp.concatenate([row_pad(w_in_t[n_main:n_main + n_heads_b]),
                                      row_pad(w_in_t[n_main + n_heads_b:])], axis=0),
            gate_bias=jnp.concatenate([lane_pad(b_igate[l][None, :]), lane_pad(b_fgate[l][None, :])],
                                      axis=1).astype(F32),
            w_conv=w_conv[l], b_conv=b_conv[l][None, :],
            bias_pairs=_bias_pairs(rel_bias[l]),
            g_attn=g_attn_norm[l][None, :], g_mlstm=g_mlstm_norm[l][None, :],
            w_out=w_out[l].astype(BF16), ln1_g=ln1_g[l][None, :], ln1_b=ln1_b[l][None, :],
            w_ffn_gate=w_ffn_gate[l], w_ffn_up=w_ffn_up[l],
            w_ffn_down=w_ffn_down[l].astype(BF16), ln2_g=ln2_g[l][None, :], ln2_b=ln2_b[l][None, :],
        )
        zero_state = (jnp.zeros((bp, n_heads_b, HEAD_DIM_B, HEAD_DIM_B), F32),
                      jnp.zeros((bp, n_heads_b, HEAD_DIM_B), F32),
                      jnp.zeros((bp, n_heads_b), F32))
        zero_conv = jnp.zeros((bp, CONV_W - 1, 2 * d_b), F32)
        xp, *state_p = _trunk_layer(xp, None, None, zero_conv, zero_state, w, alpha)
        new_p.append(tuple(state_p))
        bs = xs.shape[0]
        ck = cache_attn_k[l].astype(F32).reshape(bs, -1, d_attn)
        cv = cache_attn_v[l].astype(F32).reshape(bs, -1, d_attn)
        assert ck.shape[1] == BAND_ROWS
        xs, *state_s = _trunk_layer(xs, ck, cv, state_conv[l],
                                    (state_mlstm_C[l], state_mlstm_n[l], state_mlstm_m[l]), w, alpha)
        new_s.append(tuple(state_s))
    k_p, v_p, conv_p, C_p, n_p, m_p = [jnp.stack(t) for t in zip(*new_p)]
    k_s, v_s, conv_s, C_s, n_s, m_s = [jnp.stack(t) for t in zip(*new_s)]
    return (xp, xs, k_p, v_p, conv_p, C_p, n_p, m_p, k_s, v_s, conv_s, C_s, n_s, m_s)
```

```python
import functools

import jax
import jax.numpy as jnp
from jax import lax
from jax.experimental import pallas as pl
from jax.experimental.pallas import tpu as pltpu

CHUNK = 64
BAND_ROWS = 8 * CHUNK
BAND = BAND_ROWS + CHUNK
REL_CLIP = 256
HEAD_DIM_A = 64
HEAD_DIM_B = 128
CONV_W = 4
LANES = 128
SUBLANES = 8
MXU_COLS = 256
NEG_BIG = -1e30
LOG2_E = 1.4426950408889634
LN_SPLITS = 4
ATTN_CHUNKS_PER_STEP = 8
VMEM_LIMIT_BYTES = 56 * 1024 * 1024

F32 = jnp.float32
BF16 = jnp.bfloat16


def _params(*semantics, flags=None, fuse_inputs=None):
    return pltpu.CompilerParams(dimension_semantics=semantics, vmem_limit_bytes=VMEM_LIMIT_BYTES,
                                flags=flags, allow_input_fusion=fuse_inputs)


def _mm(a, b):
    return jnp.dot(a, b, preferred_element_type=F32)


def _mm_nt(a, b, precision=None):
    return lax.dot_general(a, b, (((1,), (1,)), ((), ())), precision=precision,
                           preferred_element_type=F32)


def _in_proj_kernel(x_ref, wt_ref, wgt_ref, z_ref, g_ref, xb_ref):
    @pl.when(pl.program_id(1) == 0)
    def _():
        xb = x_ref[...].astype(BF16)
        xb_ref[...] = xb
        g_ref[...] = _mm_nt(xb, wgt_ref[...].astype(BF16))

    z_ref[...] = _mm_nt(xb_ref[...], wt_ref[...]).astype(z_ref.dtype)


def _in_proj(x2d, w_main_t, w_gate_t, bm, bn):
    t, d = x2d.shape
    n = w_main_t.shape[0]
    ng = w_gate_t.shape[0]
    return pl.pallas_call(
        _in_proj_kernel,
        grid=(t // bm, n // bn),
        in_specs=[pl.BlockSpec((bm, d), lambda i, j: (i, 0)),
                  pl.BlockSpec((bn, d), lambda i, j: (j, 0)),
                  pl.BlockSpec((ng, d), lambda i, j: (0, 0))],
        out_specs=[pl.BlockSpec((bm, bn), lambda i, j: (i, j)),
                   pl.BlockSpec((bm, ng), lambda i, j: (i, 0))],
        out_shape=[jax.ShapeDtypeStruct((t, n), BF16),
                   jax.ShapeDtypeStruct((t, ng), F32)],
        scratch_shapes=[pltpu.VMEM((bm, d), BF16)],
        compiler_params=_params("parallel", "arbitrary"),
        name="in_proj",
    )(x2d, w_main_t, w_gate_t)


def _attn_kernel(*refs, has_hist, n_pairs, n_sub):
    if has_hist:
        q_ref, k_ref, v_ref, kh_ref, vh_ref, bias_ref, g_ref, o_ref, kt_ref, vt_ref, kpad, vpad = refs
    else:
        q_ref, k_ref, v_ref, bias_ref, g_ref, o_ref, kt_ref, vt_ref, kpad, vpad = refs
    step = pl.program_id(1)
    s_len = k_ref.shape[1]
    n_keep = kt_ref.shape[1]
    n_new = min(n_keep, s_len)

    @pl.when(step == 0)
    def _():
        if has_hist:
            kpad[0:BAND_ROWS, :] = kh_ref[0].astype(BF16)
            vpad[0:BAND_ROWS, :] = vh_ref[0].astype(BF16)
            if n_new < n_keep:
                kt_ref[0, 0:n_keep - n_new, :] = kh_ref[0, BAND_ROWS - (n_keep - n_new):, :]
                vt_ref[0, 0:n_keep - n_new, :] = vh_ref[0, BAND_ROWS - (n_keep - n_new):, :]
        else:
            kpad[0:BAND_ROWS, :] = jnp.zeros((BAND_ROWS, kpad.shape[1]), BF16)
            vpad[0:BAND_ROWS, :] = jnp.zeros((BAND_ROWS, vpad.shape[1]), BF16)
        kpad[BAND_ROWS:BAND_ROWS + s_len, :] = k_ref[0]
        vpad[BAND_ROWS:BAND_ROWS + s_len, :] = v_ref[0]
        kt_ref[0, n_keep - n_new:, :] = k_ref[0, s_len - n_new:, :].astype(F32)
        vt_ref[0, n_keep - n_new:, :] = v_ref[0, s_len - n_new:, :].astype(F32)

    def chunks(masked):
        for u in range(n_sub):
            _attn_chunk(q_ref, kpad, vpad, bias_ref, g_ref, o_ref, step * n_sub + u, u, n_pairs, masked)

    if has_hist:
        chunks(masked=False)
    else:
        first_steps = BAND_ROWS // CHUNK // n_sub
        pl.when(step < first_steps)(functools.partial(chunks, masked=True))
        pl.when(step >= first_steps)(functools.partial(chunks, masked=False))


def _attn_chunk(q_ref, kpad, vpad, bias_ref, g_ref, o_ref, c, u, n_pairs, masked):
    start = pl.multiple_of(c * CHUNK, CHUNK)
    rows = slice(u * CHUNK, (u + 1) * CHUNK)
    lane = lax.broadcasted_iota(jnp.int32, (CHUNK, LANES), 1)
    row2 = lax.broadcasted_iota(jnp.int32, (2 * CHUNK, LANES), 0)
    lane2 = lax.broadcasted_iota(jnp.int32, (2 * CHUNK, LANES), 1)
    own_head = (row2 >= CHUNK) == (lane2 >= HEAD_DIM_A)
    if masked:
        col = lax.broadcasted_iota(jnp.int32, (2 * CHUNK, BAND), 1)
        valid = col >= BAND_ROWS - c * CHUNK

    def scores(p):
        cols = slice(p * LANES, (p + 1) * LANES)
        qp = q_ref[0, rows, cols]
        zero = jnp.zeros_like(qp)
        q2 = jnp.concatenate([jnp.where(lane < HEAD_DIM_A, qp, zero),
                              jnp.where(lane >= HEAD_DIM_A, qp, zero)], axis=0)
        return _mm_nt(q2, kpad[pl.ds(start, BAND), cols])

    pairs = range(n_pairs)
    s = [scores(p) + bias_ref[p] for p in pairs]
    if masked:
        s = [jnp.where(valid, s[p], NEG_BIG) for p in pairs]
    m = [jnp.max(s[p], axis=-1, keepdims=True) for p in pairs]
    e = [jnp.exp2((s[p] - m[p]).astype(BF16)) for p in pairs]
    ones = jnp.ones((BAND, LANES), BF16)
    o = [_mm(e[p], jnp.concatenate([vpad[pl.ds(start, BAND), p * LANES:(p + 1) * LANES], ones], axis=1))
         for p in pairs]
    o = [jnp.where(own_head, o[p][:, :LANES] / o[p][:, LANES:], 0.0) for p in pairs]
    ssq = [jnp.sum(o[p] * o[p], axis=-1, keepdims=True) for p in pairs]
    for p in pairs:
        cols = slice(p * LANES, (p + 1) * LANES)
        on = o[p] * lax.rsqrt(ssq[p] * (1.0 / HEAD_DIM_A) + 1e-6)
        o_ref[0, rows, cols] = ((on[:CHUNK] + on[CHUNK:]) * g_ref[:, cols]).astype(o_ref.dtype)


def _attention(z3, hist_k, hist_v, bias_pairs, g_attn):
    b, s_len, _ = z3.shape
    d_attn = g_attn.shape[1]
    n_pairs = d_attn // LANES
    has_hist = hist_k is not None
    n_keep = BAND_ROWS if has_hist else min(BAND_ROWS, s_len)
    n_chunks = s_len // CHUNK
    n_sub = ATTN_CHUNKS_PER_STEP if n_chunks % ATTN_CHUNKS_PER_STEP == 0 else 1
    assert (BAND_ROWS // CHUNK) % n_sub == 0
    in_specs = [pl.BlockSpec((1, n_sub * CHUNK, d_attn), lambda i, c: (i, c, 0)),
                pl.BlockSpec((1, s_len, d_attn), lambda i, c: (i, 0, 1)),
                pl.BlockSpec((1, s_len, d_attn), lambda i, c: (i, 0, 2))]
    args = [z3, z3, z3]
    if has_hist:
        in_specs += [pl.BlockSpec((1, BAND_ROWS, d_attn), lambda i, c: (i, 0, 0))] * 2
        args += [hist_k, hist_v]
    in_specs += [pl.BlockSpec(bias_pairs.shape, lambda i, c: (0, 0, 0)),
                 pl.BlockSpec((1, d_attn), lambda i, c: (0, 0))]
    args += [bias_pairs, g_attn]
    tail_spec = pl.BlockSpec((1, n_keep, d_attn), lambda i, c: (i, 0, 0))
    tail_shape = jax.ShapeDtypeStruct((b, n_keep, d_attn), F32)
    return pl.pallas_call(
        functools.partial(_attn_kernel, has_hist=has_hist, n_pairs=n_pairs, n_sub=n_sub),
        grid=(b, n_chunks // n_sub),
        in_specs=in_specs,
        out_specs=[pl.BlockSpec((1, n_sub * CHUNK, d_attn), lambda i, c: (i, c, 0)), tail_spec, tail_spec],
        out_shape=[jax.ShapeDtypeStruct((b, s_len, d_attn), BF16), tail_shape, tail_shape],
        scratch_shapes=[pltpu.VMEM((BAND_ROWS + s_len, d_attn), BF16)] * 2,
        compiler_params=_params("parallel", "arbitrary"),
        name="attn_hist" if has_hist else "attn",
    )(*args)


def _bias_pairs(rel_bias):
    n_heads = rel_bias.shape[0]
    n_far = BAND - REL_CLIP
    far = jnp.broadcast_to(rel_bias[:, 2 * REL_CLIP:], (n_heads, n_far))
    near = rel_bias[:, REL_CLIP - (CHUNK - 1):2 * REL_CLIP][:, ::-1]
    ext = jnp.concatenate([far, near], axis=1)
    n_ext = BAND + CHUNK - 1
    flat = jnp.tile(jnp.pad(ext, ((0, 0), (0, 1))), (1, CHUNK))
    bias = flat[:, CHUNK - 1:CHUNK - 1 + CHUNK * n_ext].reshape(n_heads, CHUNK, n_ext)[:, :, :BAND]
    return (bias * LOG2_E).reshape(n_heads // 2, 2 * CHUNK, BAND).astype(F32)


MLSTM_BLK = 128
STATE_ROWS = HEAD_DIM_B + 16


def _sigmoid(x):
    return 0.5 * jnp.tanh(0.5 * x) + 0.5


def _log_sigmoid(x):
    return jnp.minimum(x, 0.0) - jnp.log1p(jnp.exp(-jnp.abs(x)))


def _mm_tn(a, b):
    return lax.dot_general(a, b, (((0,), (0,)), ((), ())), preferred_element_type=F32)


def _mlstm_kernel(q_ref, k_ref, v_ref, og_ref, gates_ref, conv0_ref, c0_ref, n0_ref, m0_ref,
                  wconv_ref, bconv_ref, gbias_ref, gnorm_ref,
                  o_ref, c_out_ref, n_out_ref, m_out_ref,
                  xs, qb_s, k_s, ct_s, m_s, *, n_heads, valid_len):
    blk = MLSTM_BLK
    hd = HEAD_DIM_B
    c = pl.program_id(1)
    last = pl.num_programs(1) - 1
    d_b = n_heads * hd
    pad = SUBLANES
    heads = range(n_heads)

    @pl.when(c == 0)
    def _():
        xs[0:pad, :] = conv0_ref[0]
        for h in heads:
            ct_s[h, 0:hd, :] = c0_ref[0, h].T
            ct_s[h, hd:, :] = jnp.zeros((STATE_ROWS - hd, hd), F32)
            ct_s[h, hd:hd + 1, :] = n0_ref[0, h:h + 1, :]
        m_s[...] = m0_ref[0]

    def rows_to_blk(a):
        if valid_len == blk:
            return a
        return jnp.concatenate([a, jnp.zeros((blk - valid_len,) + a.shape[1:], a.dtype)], axis=0)

    xs[pad:pad + blk, 0:d_b] = rows_to_blk(q_ref[0].astype(F32))
    xs[pad:pad + blk, d_b:2 * d_b] = rows_to_blk(k_ref[0].astype(F32))

    gates = rows_to_blk(gates_ref[0]) + gbias_ref[...]
    ig = gates[:, :LANES]
    lf = _log_sigmoid(gates[:, LANES:])
    if valid_len < blk:
        live = lax.broadcasted_iota(jnp.int32, (blk, LANES), 0) < valid_len
        ig = jnp.where(live, ig, NEG_BIG)
        lf = jnp.where(live, lf, 0.0)
    row_i = lax.broadcasted_iota(jnp.int32, (blk, blk), 0)
    col_i = lax.broadcasted_iota(jnp.int32, (blk, blk), 1)
    f_cum = jnp.dot(jnp.where(row_i >= col_i, 1.0, 0.0).astype(F32), lf,
                    precision=lax.Precision.HIGHEST, preferred_element_type=F32)
    sel_rows = -(-n_heads // SUBLANES) * SUBLANES
    sel = jnp.where(lax.broadcasted_iota(jnp.int32, (sel_rows, LANES), 0)
                    == lax.broadcasted_iota(jnp.int32, (sel_rows, LANES), 1), 1.0, 0.0).astype(F32)
    f_rows = _mm_nt(sel, f_cum, precision=lax.Precision.HIGHEST)
    imf = ig - f_cum
    m_prev = m_s[...]
    f_last = f_cum[blk - 1:blk, :]
    g = f_last + imf
    a_last = f_last + m_prev
    m_new = jnp.maximum(a_last, jnp.max(g, axis=0, keepdims=True))
    w_old = jnp.exp(a_last - m_new)
    wk = jnp.exp(g - m_new)

    for j in range(2 * n_heads):
        cols = slice(j * hd, (j + 1) * hd)
        acc = bconv_ref[:, cols]
        for tap in range(CONV_W):
            acc = acc + xs[pl.ds(pad - (CONV_W - 1) + tap, blk), cols] * wconv_ref[tap:tap + 1, cols]
        y = acc * _sigmoid(acc)
        if j < n_heads:
            qb_s[:, cols] = y.astype(BF16)
        else:
            k_s[:, (j - n_heads) * hd:(j - n_heads + 1) * hd] = y * (hd ** -0.5)

    hcols = [slice(h * hd, (h + 1) * hd) for h in heads]
    st = [_mm_nt(k_s[:, hcols[h]].astype(BF16), qb_s[:, hcols[h]]) for h in heads]
    inter = [_mm_nt(ct_s[h].astype(BF16), qb_s[:, hcols[h]]) for h in heads]

    visible = row_i <= col_i
    f_row = [f_rows[h:h + 1, :] for h in heads]
    a_row = [f_row[h] + m_prev[:, h:h + 1] for h in heads]
    dt = [jnp.where(visible, jnp.broadcast_to(imf[:, h:h + 1], (blk, blk)) + f_row[h], NEG_BIG) for h in heads]
    m_t = [jnp.maximum(a_row[h], jnp.max(dt[h], axis=0, keepdims=True)) for h in heads]
    pt = [st[h] * jnp.exp(dt[h] - m_t[h]) for h in heads]
    w_inter = [jnp.exp(a_row[h] - m_t[h]) for h in heads]
    den = [jnp.sum(pt[h], axis=0, keepdims=True) + w_inter[h] * inter[h][hd:hd + 1, :] for h in heads]
    scale = [1.0 / jnp.maximum(jnp.abs(den[h]), jnp.exp(-m_t[h])) for h in heads]
    kw = [k_s[:, hcols[h]] * jnp.broadcast_to(wk[:, h:h + 1], (blk, hd)) for h in heads]

    v = [rows_to_blk(v_ref[0, :, hcols[h]]) for h in heads]
    num_t = [_mm_tn(v[h], pt[h].astype(BF16)) for h in heads]
    dct = [_mm_tn(v[h], kw[h].astype(BF16)) for h in heads]

    ht = [(num_t[h] + w_inter[h] * inter[h][0:hd, :]) * scale[h] for h in heads]
    ssq = [jnp.sum(ht[h] * ht[h], axis=0, keepdims=True) for h in heads]
    for h in heads:
        hn = (ht[h] * lax.rsqrt(ssq[h] * (1.0 / hd) + 1e-6)).T[0:valid_len]
        out = hn * gnorm_ref[:, hcols[h]] * _sigmoid(og_ref[0, :, hcols[h]].astype(F32))
        o_ref[0, :, hcols[h]] = out.astype(o_ref.dtype)
        decay = w_old[:, h:h + 1]
        ct_s[h, 0:hd, :] = decay * ct_s[h, 0:hd, :] + dct[h]
        ct_s[h, hd:hd + 1, :] = decay * ct_s[h, hd:hd + 1, :] + jnp.sum(kw[h], axis=0, keepdims=True)
    m_s[...] = m_new
    xs[0:pad, :] = xs[blk:blk + pad, :]

    @pl.when(c == last)
    def _():
        for h in heads:
            c_out_ref[0, h] = ct_s[h, 0:hd, :].T
            n_out_ref[0, h:h + 1, :] = ct_s[h, hd:hd + 1, :]
        m_out_ref[0] = m_s[...]


def _mlstm(z3, gates3, conv0, c0, n0, m0, w_conv, b_conv, gate_bias, g_norm, col_block0):
    b, s_len, _ = z3.shape
    n_heads = c0.shape[1]
    d_b = n_heads * HEAD_DIM_B
    rows = min(s_len, MLSTM_BLK)
    assert s_len % rows == 0 and rows % (2 * SUBLANES) == 0 and n_heads <= LANES
    zspec = lambda k: pl.BlockSpec((1, rows, d_b), lambda i, c: (i, c, col_block0 + k))
    full = lambda a: pl.BlockSpec(a.shape, lambda i, c: (0,) * a.ndim)
    per_b = lambda a: pl.BlockSpec((1,) + a.shape[1:], lambda i, c: (i,) + (0,) * (a.ndim - 1))
    blk = MLSTM_BLK
    return pl.pallas_call(
        functools.partial(_mlstm_kernel, n_heads=n_heads, valid_len=rows),
        grid=(b, s_len // rows),
        in_specs=[zspec(0), zspec(1), zspec(2), zspec(3),
                  pl.BlockSpec((1, rows, 2 * LANES), lambda i, c: (i, c, 0)),
                  per_b(conv0), per_b(c0), per_b(n0), per_b(m0),
                  full(w_conv), full(b_conv), full(gate_bias), full(g_norm)],
        out_specs=[pl.BlockSpec((1, rows, d_b), lambda i, c: (i, c, 0)),
                   per_b(c0), per_b(n0), per_b(m0)],
        out_shape=[jax.ShapeDtypeStruct((b, s_len, d_b), BF16),
                   jax.ShapeDtypeStruct(c0.shape, F32),
                   jax.ShapeDtypeStruct(n0.shape, F32),
                   jax.ShapeDtypeStruct(m0.shape, F32)],
        scratch_shapes=[pltpu.VMEM((blk + SUBLANES, 2 * d_b), F32),
                        pltpu.VMEM((blk, d_b), BF16),
                        pltpu.VMEM((blk, d_b), F32),
                        pltpu.VMEM((n_heads, STATE_ROWS, HEAD_DIM_B), F32),
                        pltpu.VMEM((1, LANES), F32)],
        compiler_params=_params("parallel", "arbitrary"),
        name="mlstm",
    )(z3, z3, z3, z3, gates3, conv0, c0, n0, m0, w_conv, b_conv, gate_bias, g_norm)


def _layer_norm(r, g, b):
    mu = jnp.mean(r, axis=-1, keepdims=True)
    rc = r - mu
    var = jnp.mean(rc * rc, axis=-1, keepdims=True)
    return rc * lax.rsqrt(var + 1e-5) * g + b


def _row_splits(n_rows):
    sub = n_rows // LN_SPLITS if n_rows % (LN_SPLITS * 2 * SUBLANES) == 0 else n_rows
    return [slice(r, r + sub) for r in range(0, n_rows, sub)]


def _out_proj_kernel(a_ref, m_ref, x_ref, wa_ref, wm_ref, g_ref, b_ref, x1_ref, x1b_ref, *, alpha):
    for rows in _row_splits(x_ref.shape[0]):
        mix = _mm(a_ref[rows, :], wa_ref[...]) + _mm(m_ref[rows, :], wm_ref[...])
        y = _layer_norm(alpha * x_ref[rows, :] + mix, g_ref[...], b_ref[...])
        x1_ref[rows, :] = y
        x1b_ref[rows, :] = y.astype(BF16)


def _out_proj(attn2d, mlstm2d, x2d, w_out, ln_g, ln_b, alpha, bm):
    t, d = x2d.shape
    da, db = attn2d.shape[1], mlstm2d.shape[1]
    assert da == db
    row = lambda w: pl.BlockSpec((bm, w), lambda i: (i, 0))
    vec = pl.BlockSpec((1, d), lambda i: (0, 0))
    return pl.pallas_call(
        functools.partial(_out_proj_kernel, alpha=alpha),
        grid=(t // bm,),
        in_specs=[row(da), row(db), row(d),
                  pl.BlockSpec((da, d), lambda i: (0, 0)),
                  pl.BlockSpec((db, d), lambda i: (1, 0)),
                  vec, vec],
        out_specs=[row(d), row(d)],
        out_shape=[jax.ShapeDtypeStruct((t, d), F32), jax.ShapeDtypeStruct((t, d), BF16)],
        compiler_params=_params("parallel", fuse_inputs=[False, False, False, True, True, False, False]),
        name="out_proj_ln",
    )(attn2d, mlstm2d, x2d, w_out, w_out, ln_g, ln_b)


def _ffn_up_kernel(x_ref, wg_ref, wu_ref, h_ref, wgb_ref, wub_ref):
    @pl.when(pl.program_id(1) == 0)
    def _():
        wgb_ref[...] = wg_ref[...].astype(BF16)
        wub_ref[...] = wu_ref[...].astype(BF16)

    x = x_ref[...]
    gate = _mm(x, wgb_ref[...])
    h_ref[...] = (gate * _sigmoid(gate) * _mm(x, wub_ref[...])).astype(h_ref.dtype)


def _ffn_up(x1b, w_gate, w_up, bm, bf):
    t, d = x1b.shape
    f = w_gate.shape[1]
    wspec = pl.BlockSpec((d, bf), lambda j, i: (0, j))
    return pl.pallas_call(
        _ffn_up_kernel,
        grid=(f // bf, t // bm),
        in_specs=[pl.BlockSpec((bm, d), lambda j, i: (i, 0)), wspec, wspec],
        out_specs=pl.BlockSpec((bm, bf), lambda j, i: (i, j)),
        out_shape=jax.ShapeDtypeStruct((t, f), BF16),
        scratch_shapes=[pltpu.VMEM((d, bf), BF16)] * 2,
        compiler_params=_params("parallel", "arbitrary"),
        name="ffn_up",
    )(x1b, w_gate, w_up)


def _ffn_down_kernel(h_ref, wd_ref, x1_ref, g_ref, b_ref, y_ref, *, alpha):
    for rows in _row_splits(x1_ref.shape[0]):
        y_ref[rows, :] = _layer_norm(alpha * x1_ref[rows, :] + _mm(h_ref[rows, :], wd_ref[...]),
                                     g_ref[...], b_ref[...])


def _ffn_down(h, w_down, x1, ln_g, ln_b, alpha, bm):
    t, f = h.shape
    d = w_down.shape[1]
    vec = pl.BlockSpec((1, d), lambda i: (0, 0))
    return pl.pallas_call(
        functools.partial(_ffn_down_kernel, alpha=alpha),
        grid=(t // bm,),
        in_specs=[pl.BlockSpec((bm, f), lambda i: (i, 0)),
                  pl.BlockSpec((f, d), lambda i: (0, 0), pipeline_mode=pl.Buffered(1)),
                  pl.BlockSpec((bm, d), lambda i: (i, 0)),
                  vec, vec],
        out_specs=pl.BlockSpec((bm, d), lambda i: (i, 0)),
        out_shape=jax.ShapeDtypeStruct((t, d), F32),
        compiler_params=_params("parallel", fuse_inputs=[False, True, False, False, False]),
        name="ffn_down_ln",
    )(h, w_down, x1, ln_g, ln_b)


def _largest_divisor(n, cap):
    return max(d for d in range(1, min(n, cap) + 1) if n % d == 0)


def _tiles(t, n_main, d_ff):
    return dict(
        rows=_largest_divisor(t, 1024),
        rows_ln=_largest_divisor(t, 512),
        in_proj_cols=_largest_divisor(n_main // MXU_COLS, 7) * MXU_COLS,
        ffn_cols=_largest_divisor(d_ff // MXU_COLS, 2) * MXU_COLS,
    )


def _trunk_layer(x, hist_k, hist_v, conv_state, mlstm_state, w, alpha):
    b, s_len, d = x.shape
    t = b * s_len
    c0, n0, m0 = mlstm_state
    n_heads_b = c0.shape[1]
    d_b = n_heads_b * HEAD_DIM_B
    d_attn = w["g_attn"].shape[1]
    n_main = 3 * d_attn + 4 * d_b
    assert d_attn == d_b and w["w_main_t"].shape[0] == n_main

    tiles = _tiles(t, n_main, w["w_ffn_gate"].shape[1])
    x2d = x.reshape(t, d)
    z, gates = _in_proj(x2d, w["w_main_t"], w["w_gate_t"], tiles["rows"], tiles["in_proj_cols"])
    z3 = z.reshape(b, s_len, -1)

    attn, k_tail, v_tail = _attention(z3, hist_k, hist_v, w["bias_pairs"], w["g_attn"])

    conv0 = jnp.pad(conv_state.astype(F32), ((0, 0), (SUBLANES - (CONV_W - 1), 0), (0, 0)))
    m0_row = jnp.pad(m0.astype(F32)[:, None, :], ((0, 0), (0, 0), (0, LANES - n_heads_b)))
    mlstm, c_new, n_new, m_new = _mlstm(
        z3, gates.reshape(b, s_len, 2 * LANES), conv0, c0.astype(F32), n0.astype(F32), m0_row,
        w["w_conv"], w["b_conv"], w["gate_bias"], w["g_mlstm"], 3)

    x1, x1b = _out_proj(attn.reshape(t, d_attn), mlstm.reshape(t, d_b), x2d, w["w_out"],
                        w["ln1_g"], w["ln1_b"], alpha, tiles["rows_ln"])
    hmid = _ffn_up(x1b, w["w_ffn_gate"], w["w_ffn_up"], tiles["rows"], tiles["ffn_cols"])
    y = _ffn_down(hmid, w["w_ffn_down"], x1, w["ln2_g"], w["ln2_b"], alpha, tiles["rows_ln"])

    n_heads_a = d_attn // HEAD_DIM_A
    to_heads = lambda a: a.reshape(a.shape[0], a.shape[1], n_heads_a, HEAD_DIM_A)
    qk_raw_tail = z3[:, s_len - (CONV_W - 1):, 3 * d_attn:3 * d_attn + 2 * d_b].astype(F32)
    return (y.reshape(b, s_len, d), to_heads(k_tail), to_heads(v_tail), qk_raw_tail,
            c_new, n_new, m_new[:, 0, :n_heads_b])


def kernel(x_prompt, x_sample, cache_attn_k, cache_attn_v, state_conv, state_mlstm_C, state_mlstm_n, state_mlstm_m, w_in, b_igate, b_fgate, w_conv, b_conv, rel_bias, g_attn_norm, g_mlstm_norm, w_out, ln1_g, ln1_b, w_ffn_gate, w_ffn_up, w_ffn_down, ln2_g, ln2_b):
    depth = w_in.shape[0]
    alpha = (2.0 * depth) ** 0.25
    xp, xs = x_prompt, x_sample
    bp, sp, d_model = xp.shape
    n_heads_b = b_igate.shape[1]
    d_b = n_heads_b * HEAD_DIM_B
    d_attn = g_attn_norm.shape[1]
    n_heads_a = d_attn // HEAD_DIM_A
    n_main = 3 * d_attn + 4 * d_b
    new_p, new_s = [], []
    for l in range(depth):
        q_scale = jnp.where(jnp.arange(n_main) < d_attn, (HEAD_DIM_A ** -0.5) * LOG2_E, 1.0).astype(F32)
        w_in_t = jnp.swapaxes(w_in[l], 0, 1)
        lane_pad = lambda a: jnp.pad(a, ((0, 0), (0, LANES - n_heads_b)))
        row_pad = lambda a: jnp.pad(a, ((0, LANES - n_heads_b), (0, 0)))
        w = dict(
            w_main_t=(w_in_t[:n_main] * q_scale[:, None]).astype(BF16),
            w_gate_t=jnp.concatenate([row_pad(w_in_t[n_main:n_main + n_heads_b]),
                                      row_pad(w_in_t[n_main + n_heads_b:])], axis=0),
            gate_bias=jnp.concatenate([lane_pad(b_igate[l][None, :]), lane_pad(b_fgate[l][None, :])],
                                      axis=1).astype(F32),
            w_conv=w_conv[l], b_conv=b_conv[l][None, :],
            bias_pairs=_bias_pairs(rel_bias[l]),
            g_attn=g_attn_norm[l][None, :], g_mlstm=g_mlstm_norm[l][None, :],
            w_out=w_out[l].astype(BF16), ln1_g=ln1_g[l][None, :], ln1_b=ln1_b[l][None, :],
            w_ffn_gate=w_ffn_gate[l], w_ffn_up=w_ffn_up[l],
            w_ffn_down=w_ffn_down[l].astype(BF16), ln2_g=ln2_g[l][None, :], ln2_b=ln2_b[l][None, :],
        )
        zero_state = (jnp.zeros((bp, n_heads_b, HEAD_DIM_B, HEAD_DIM_B), F32),
                      jnp.zeros((bp, n_heads_b, HEAD_DIM_B), F32),
                      jnp.zeros((bp, n_heads_b), F32))
        zero_conv = jnp.zeros((bp, CONV_W - 1, 2 * d_b), F32)
        xp, *state_p = _trunk_layer(xp, None, None, zero_conv, zero_state, w, alpha)
        new_p.append(tuple(state_p))
        bs = xs.shape[0]
        ck = cache_attn_k[l].astype(F32).reshape(bs, -1, d_attn)
        cv = cache_attn_v[l].astype(F32).reshape(bs, -1, d_attn)
        assert ck.shape[1] == BAND_ROWS
        xs, *state_s = _trunk_layer(xs, ck, cv, state_conv[l],
                                    (state_mlstm_C[l], state_mlstm_n[l], state_mlstm_m[l]), w, alpha)
        new_s.append(tuple(state_s))
    k_p, v_p, conv_p, C_p, n_p, m_p = [jnp.stack(t) for t in zip(*new_p)]
    k_s, v_s, conv_s, C_s, n_s, m_s = [jnp.stack(t) for t in zip(*new_s)]
    return (xp, xs, k_p, v_p, conv_p, C_p, n_p, m_p, k_s, v_s, conv_s, C_s, n_s, m_s)
```
